```python
import jax, jax.numpy as jnp
from jax import lax
import numpy as np

D_MODEL = 1024
BATCH = 2
SEQ = 8192
DEPTH = 4
DEC_BATCH = 4
DEC_SEQ = 4096
PAST_LEN = 128

D_ATTN = 512
D_CONV = 512
N_HEADS = 8
QK_NOPE_DIM = 64
QK_ROPE_DIM = 32
V_HEAD_DIM = 64
QK_HEAD_DIM = QK_NOPE_DIM + QK_ROPE_DIM
Q_LORA_RANK = 256
KV_LORA_RANK = 128
ROPE_THETA = 10000.0
CONV_WIDTH = 31
CONV_PAD = (CONV_WIDTH - 1) // 2
D_FF = 2816
N_EXPERTS = 8
TOP_K = 2
D_EXPERT = 1024
Q_BLOCK = 128
EPS = 1e-6
N_DENSE = (DEPTH + 1) // 2
N_MOE = DEPTH // 2
OFF_CQ = 2 * D_CONV
OFF_CKV = OFF_CQ + Q_LORA_RANK
OFF_KR = OFF_CKV + KV_LORA_RANK
IN_COLS = OFF_KR + QK_ROPE_DIM

kernel_name = "hymba_conformer_mla_moe_encoder"


def rmsnorm(x, g):
    xf = x.astype(jnp.float32)
    y = xf * lax.rsqrt(jnp.mean(xf * xf, axis=-1, keepdims=True) + EPS)
    return (y * g.astype(jnp.float32)).astype(x.dtype)


def layernorm(x, g, b):
    xf = x.astype(jnp.float32)
    mu = jnp.mean(xf, axis=-1, keepdims=True)
    var = jnp.mean(jnp.square(xf - mu), axis=-1, keepdims=True)
    y = (xf - mu) * lax.rsqrt(var + EPS)
    return (y * g.astype(jnp.float32) + b.astype(jnp.float32)).astype(x.dtype)


def rope_tables(seq_len):
    pos = jnp.arange(seq_len, dtype=jnp.float32)
    freqs = ROPE_THETA ** (-(jnp.arange(0, QK_ROPE_DIM, 2, dtype=jnp.float32) / QK_ROPE_DIM))
    ang = pos[:, None] * freqs[None, :]
    ang = jnp.concatenate([ang, ang], axis=-1)
    return jnp.cos(ang), jnp.sin(ang)


def apply_rope(x, cos, sin):
    xf = x.astype(jnp.float32)
    x1, x2 = jnp.split(xf, 2, axis=-1)
    rot = jnp.concatenate([-x2, x1], axis=-1)
    return (xf * cos + rot * sin).astype(x.dtype)


def mla_group(c_q, c_kv, k_rope, q_norm_g, w_uq, kv_norm_g, w_ukv):
    B, S, _ = c_q.shape
    q = (rmsnorm(c_q, q_norm_g) @ w_uq).reshape(B, S, N_HEADS, QK_HEAD_DIM)
    q_nope, q_rope = q[..., :QK_NOPE_DIM], q[..., QK_NOPE_DIM:]
    kv = (rmsnorm(c_kv, kv_norm_g) @ w_ukv).reshape(B, S, N_HEADS, QK_NOPE_DIM + V_HEAD_DIM)
    k_nope, v = kv[..., :QK_NOPE_DIM], kv[..., QK_NOPE_DIM:]
    cos, sin = rope_tables(S)
    q_rope = apply_rope(q_rope, cos[None, :, None, :], sin[None, :, None, :])
    k_rope = apply_rope(k_rope, cos[None], sin[None])
    scale = QK_HEAD_DIM ** -0.5

    def block(i):
        start = i * Q_BLOCK
        qn = lax.dynamic_slice_in_dim(q_nope, start, Q_BLOCK, axis=1)
        qr = lax.dynamic_slice_in_dim(q_rope, start, Q_BLOCK, axis=1)
        s = (jnp.einsum('bqhd,bkhd->bhqk', qn, k_nope)
             + jnp.einsum('bqhr,bkr->bhqk', qr, k_rope))
        p = jax.nn.softmax(s.astype(jnp.float32) * scale, axis=-1).astype(v.dtype)
        return jnp.einsum('bhqk,bkhd->bqhd', p, v)

    o = lax.map(block, jnp.arange(S // Q_BLOCK))
    return jnp.transpose(o, (1, 0, 2, 3, 4)).reshape(B, S, N_HEADS * V_HEAD_DIM)


def conv_group(u, dw_w, dw_b, ln_g, ln_b):
    a, g = u[..., :D_CONV], u[..., D_CONV:]
    h = a * jax.nn.sigmoid(g)
    h = lax.conv_general_dilated(h, dw_w.astype(h.dtype), window_strides=(1,),
                                 padding=[(CONV_PAD, CONV_PAD)],
                                 dimension_numbers=('NWC', 'WIO', 'NWC'),
                                 feature_group_count=D_CONV) + dw_b
    h = layernorm(h, ln_g, ln_b)
    return jax.nn.silu(h)


def swiglu(h, wg, wu, wd):
    return (jax.nn.silu(h @ wg) * (h @ wu)) @ wd


def moe(h, w_router, b_router, wg, wu, wd):
    B, S, D = h.shape
    t = h.reshape(B * S, D)
    logits = (t @ w_router).astype(jnp.float32) + b_router.astype(jnp.float32)
    probs = jax.nn.softmax(logits, axis=-1)
    topv, topi = lax.top_k(probs, TOP_K)
    topv = topv / jnp.sum(topv, axis=-1, keepdims=True)
    gates = jnp.sum(jax.nn.one_hot(topi, N_EXPERTS, dtype=jnp.float32) * topv[..., None], axis=1)
    gates = gates.astype(t.dtype)
    y = jnp.zeros_like(t)
    for e in range(N_EXPERTS):
        y = y + gates[:, e:e + 1] * swiglu(t, wg[e], wu[e], wd[e])
    return y.reshape(B, S, D)


def trunk(x, attn_norm_g, w_in, q_norm_g, w_uq, kv_norm_g, w_ukv, conv_dw_w, conv_dw_b,
          conv_ln_g, conv_ln_b, attn_out_norm_g, conv_out_norm_g, w_out, ffn_norm_g,
          dense_w_gate, dense_w_up, dense_w_down, moe_w_router, moe_b_router,
          moe_w_gate, moe_w_up, moe_w_down, final_norm_g):
    for i in range(DEPTH):
        h = rmsnorm(x, attn_norm_g[i])
        z = h @ w_in[i]
        u_conv = z[..., :OFF_CQ]
        c_q = z[..., OFF_CQ:OFF_CKV]
        c_kv = z[..., OFF_CKV:OFF_KR]
        k_r = z[..., OFF_KR:]
        o_a = mla_group(c_q, c_kv, k_r, q_norm_g[i], w_uq[i], kv_norm_g[i], w_ukv[i])
        o_c = conv_group(u_conv, conv_dw_w[i], conv_dw_b[i], conv_ln_g[i], conv_ln_b[i])
        o = jnp.concatenate([rmsnorm(o_a, attn_out_norm_g[i]),
                             rmsnorm(o_c, conv_out_norm_g[i])], axis=-1)
        x = x + o @ w_out[i]
        h = rmsnorm(x, ffn_norm_g[i])
        j = i // 2
        if i % 2 == 0:
            x = x + swiglu(h, dense_w_gate[j], dense_w_up[j], dense_w_down[j])
        else:
            x = x + moe(h, moe_w_router[j], moe_b_router[j], moe_w_gate[j], moe_w_up[j], moe_w_down[j])
    return rmsnorm(x, final_norm_g)


def setup_inputs(seed: int = 0) -> dict:
    key = jax.random.key(seed)
    ks = iter(jax.random.split(key, 32))
    f32 = jnp.float32

    def nrm(shape, fan_in):
        return jax.random.normal(next(ks), shape, f32) * (fan_in ** -0.5)

    def gain(shape):
        return 1.0 + 0.02 * jax.random.normal(next(ks), shape, f32)

    def bias(shape, s=0.02):
        return s * jax.random.normal(next(ks), shape, f32)

    return {
        "x_prompt": jax.random.normal(next(ks), (BATCH, SEQ, D_MODEL), f32),
        "x_sample": jax.random.normal(next(ks), (DEC_BATCH, DEC_SEQ, D_MODEL), f32),
        "attn_norm_g": gain((DEPTH, D_MODEL)),
        "w_in": nrm((DEPTH, D_MODEL, IN_COLS), D_MODEL),
        "q_norm_g": gain((DEPTH, Q_LORA_RANK)),
        "w_uq": nrm((DEPTH, Q_LORA_RANK, N_HEADS * QK_HEAD_DIM), Q_LORA_RANK),
        "kv_norm_g": gain((DEPTH, KV_LORA_RANK)),
        "w_ukv": nrm((DEPTH, KV_LORA_RANK, N_HEADS * (QK_NOPE_DIM + V_HEAD_DIM)), KV_LORA_RANK),
        "conv_dw_w": nrm((DEPTH, CONV_WIDTH, 1, D_CONV), CONV_WIDTH),
        "conv_dw_b": bias((DEPTH, D_CONV)),
        "conv_ln_g": gain((DEPTH, D_CONV)),
        "conv_ln_b": bias((DEPTH, D_CONV)),
        "attn_out_norm_g": gain((DEPTH, D_ATTN)),
        "conv_out_norm_g": gain((DEPTH, D_CONV)),
        "w_out": nrm((DEPTH, D_ATTN + D_CONV, D_MODEL), D_ATTN + D_CONV),
        "ffn_norm_g": gain((DEPTH, D_MODEL)),
        "dense_w_gate": nrm((N_DENSE, D_MODEL, D_FF), D_MODEL),
        "dense_w_up": nrm((N_DENSE, D_MODEL, D_FF), D_MODEL),
        "dense_w_down": nrm((N_DENSE, D_FF, D_MODEL), D_FF),
        "moe_w_router": nrm((N_MOE, D_MODEL, N_EXPERTS), D_MODEL),
        "moe_b_router": bias((N_MOE, N_EXPERTS), 0.01),
        "moe_w_gate": nrm((N_MOE, N_EXPERTS, D_MODEL, D_EXPERT), D_MODEL),
        "moe_w_up": nrm((N_MOE, N_EXPERTS, D_MODEL, D_EXPERT), D_MODEL),
        "moe_w_down": nrm((N_MOE, N_EXPERTS, D_EXPERT, D_MODEL), D_EXPERT),
        "final_norm_g": gain((D_MODEL,)),
    }


def reference(x_prompt, x_sample, attn_norm_g, w_in, q_norm_g, w_uq, kv_norm_g, w_ukv,
              conv_dw_w, conv_dw_b, conv_ln_g, conv_ln_b, attn_out_norm_g, conv_out_norm_g,
              w_out, ffn_norm_g, dense_w_gate, dense_w_up, dense_w_down, moe_w_router,
              moe_b_router, moe_w_gate, moe_w_up, moe_w_down, final_norm_g):
    y_prompt = trunk(x_prompt, attn_norm_g, w_in, q_norm_g, w_uq, kv_norm_g, w_ukv, conv_dw_w,
                     conv_dw_b, conv_ln_g, conv_ln_b, attn_out_norm_g, conv_out_norm_g, w_out,
                     ffn_norm_g, dense_w_gate, dense_w_up, dense_w_down, moe_w_router,
                     moe_b_router, moe_w_gate, moe_w_up, moe_w_down, final_norm_g)
    y_sample = trunk(x_sample, attn_norm_g, w_in, q_norm_g, w_uq, kv_norm_g, w_ukv, conv_dw_w,
                     conv_dw_b, conv_ln_g, conv_ln_b, attn_out_norm_g, conv_out_norm_g, w_out,
                     ffn_norm_g, dense_w_gate, dense_w_up, dense_w_down, moe_w_router,
                     moe_b_router, moe_w_gate, moe_w_up, moe_w_down, final_norm_g)
    return (y_prompt, y_sample)
```

```python
import functools
import math

import jax
import jax.numpy as jnp
import numpy as np
from jax import lax
from jax.experimental import pallas as pl
from jax.experimental.pallas import tpu as pltpu

D_MODEL = 1024
DEPTH = 4
D_ATTN = 512
D_CONV = 512
N_HEADS = 8
QK_NOPE_DIM = 64
QK_ROPE_DIM = 32
V_HEAD_DIM = 64
QK_HEAD_DIM = QK_NOPE_DIM + QK_ROPE_DIM
Q_LORA_RANK = 256
KV_LORA_RANK = 128
ROPE_THETA = 10000.0
CONV_WIDTH = 31
CONV_PAD = (CONV_WIDTH - 1) // 2
D_FF = 2816
N_EXPERTS = 8
D_EXPERT = 1024
EPS = 1e-6
OFF_CQ = 2 * D_CONV
OFF_CKV = OFF_CQ + Q_LORA_RANK
OFF_KR = OFF_CKV + KV_LORA_RANK

LANES = 128
HEAD_PAD = LANES
HALO = 16
VMEM_LIMIT_BYTES = 56 * 1024 * 1024

BF16 = jnp.bfloat16
F32 = jnp.float32
NEG_BIG = -1e30
Q_SCALE = (QK_HEAD_DIM ** -0.5) * math.log2(math.e)


def _cparams(*sem):
    return pltpu.CompilerParams(dimension_semantics=sem, vmem_limit_bytes=VMEM_LIMIT_BYTES)


def _rms(x, g):
    return x * lax.rsqrt(jnp.mean(x * x, axis=-1, keepdims=True) + EPS) * g


def _dot(a, b):
    return jnp.dot(a, b, preferred_element_type=F32)


def _dot_nt(a, b):
    return lax.dot_general(a, b, (((1,), (1,)), ((), ())), preferred_element_type=F32)


def _pre_kernel(x_ref, g_ref, win_ref, qg_ref, wqt_ref, wqrt_ref, kvg_ref, wk_ref, wvt_ref,
                cosk_ref, sink_ref, cosq_ref, sinq_ref,
                hc_ref, q_ref, k_ref, vt_ref):
    x = x_ref[...]
    h = _rms(x, g_ref[...]).astype(BF16)
    z = _dot(h, win_ref[...])
    hc_ref[...] = z[:, :D_CONV] * jax.nn.sigmoid(z[:, D_CONV:2 * D_CONV])
    cqn = _rms(z[:, OFF_CQ:OFF_CKV], qg_ref[...]).astype(BF16)
    ckvn = _rms(z[:, OFF_CKV:OFF_KR], kvg_ref[...]).astype(BF16)
    kr = (z[:, OFF_KR:OFF_KR + LANES] * cosk_ref[...]
          + z[:, OFF_KR + LANES:OFF_KR + 2 * LANES] * sink_ref[...])
    kall = _dot(ckvn, wk_ref[...])
    for hd in range(N_HEADS):
        sl = slice(hd * HEAD_PAD, (hd + 1) * HEAD_PAD)
        k_ref[:, sl] = (kall[:, sl] + kr).astype(BF16)
    vt_ref[...] = _dot_nt(wvt_ref[...], ckvn).astype(BF16)
    qt = _dot_nt(wqt_ref[...], cqn)
    qrt = _dot_nt(wqrt_ref[...], cqn)
    cq = cosq_ref[...]
    sq = sinq_ref[...]
    zeros = jnp.zeros((HEAD_PAD - QK_HEAD_DIM, x.shape[0]), BF16)
    for hd in range(N_HEADS):
        base = hd * HEAD_PAD
        q_ref[hd, 0:QK_NOPE_DIM, :] = (qt[base:base + QK_NOPE_DIM] * Q_SCALE).astype(BF16)
        rope = (qt[base + QK_NOPE_DIM:base + QK_HEAD_DIM] * cq
                + qrt[hd * QK_ROPE_DIM:(hd + 1) * QK_ROPE_DIM] * sq)
        q_ref[hd, QK_NOPE_DIM:QK_HEAD_DIM, :] = rope.astype(BF16)
        q_ref[hd, QK_HEAD_DIM:HEAD_PAD, :] = zeros


def _pre_call(x, lw, tabs, segs, tm):
    T = x.shape[0]
    n_tiles = T // tm
    pos_map = _pos_block_map(segs, tm)
    full = lambda shape: pl.BlockSpec(shape, lambda i: (0,) * len(shape))
    in_cols = lw["w_in"].shape[1]
    return pl.pallas_call(
        _pre_kernel,
        grid=(n_tiles,),
        in_specs=[
            pl.BlockSpec((tm, D_MODEL), lambda i: (i, 0)),
            full((1, D_MODEL)),
            full((D_MODEL, in_cols)),
            full((1, Q_LORA_RANK)),
            full((N_HEADS * HEAD_PAD, Q_LORA_RANK)),
            full((N_HEADS * QK_ROPE_DIM, Q_LORA_RANK)),
            full((1, KV_LORA_RANK)),
            full((KV_LORA_RANK, N_HEADS * HEAD_PAD)),
            full((N_HEADS * V_HEAD_DIM, KV_LORA_RANK)),
            pl.BlockSpec((tm, LANES), lambda i: (pos_map(i), 0)),
            pl.BlockSpec((tm, LANES), lambda i: (pos_map(i), 0)),
            pl.BlockSpec((QK_ROPE_DIM, tm), lambda i: (0, pos_map(i))),
            pl.BlockSpec((QK_ROPE_DIM, tm), lambda i: (0, pos_map(i))),
        ],
        out_specs=[
            pl.BlockSpec((tm, D_CONV), lambda i: (i, 0)),
            pl.BlockSpec((N_HEADS, HEAD_PAD, tm), lambda i: (0, 0, i)),
            pl.BlockSpec((tm, N_HEADS * HEAD_PAD), lambda i: (i, 0)),
            pl.BlockSpec((N_HEADS * V_HEAD_DIM, tm), lambda i: (0, i)),
        ],
        out_shape=[
            jax.ShapeDtypeStruct((T, D_CONV), F32),
            jax.ShapeDtypeStruct((N_HEADS, HEAD_PAD, T), BF16),
            jax.ShapeDtypeStruct((T, N_HEADS * HEAD_PAD), BF16),
            jax.ShapeDtypeStruct((N_HEADS * V_HEAD_DIM, T), BF16),
        ],
        compiler_params=_cparams("parallel"),
        name="layer_front",
    )(x, lw["attn_norm_g"], lw["w_in"], lw["q_norm_g"], lw["wq_t"], lw["wqrot_t"],
      lw["kv_norm_g"], lw["wk"], lw["wv_t"], tabs["cosk"], tabs["sink"], tabs["cosq_t"],
      tabs["sinq_t"])


def _pos_block_map(segs, tile):
    def pos_map(i):
        out = None
        for off, n_seq, seq_len in reversed(segs):
            blk = (i - off // tile) % (seq_len // tile)
            out = blk if out is None else jnp.where(i < (off + n_seq * seq_len) // tile, blk, out)
        return out
    return pos_map


def _attn_kernel(q_ref, k_ref, vt_ref, o_ref, *, tk):
    q = q_ref[0]
    tq = q.shape[1]
    n_kv = k_ref.shape[0] // tk

    def body(j, carry):
        m, l, acc = carry
        start = pl.multiple_of(j * tk, tk)
        s = _dot(k_ref[pl.ds(start, tk), :], q)
        m_new = jnp.maximum(m, jnp.max(s, axis=0, keepdims=True))
        alpha = jnp.exp2(m - m_new)
        p = jnp.exp2(s - m_new)
        l = alpha * l + jnp.sum(p, axis=0, keepdims=True)
        acc = alpha * acc + _dot(vt_ref[:, pl.ds(start, tk)], p.astype(BF16))
        return m_new, l, acc

    m0 = jnp.full((1, tq), NEG_BIG, F32)
    l0 = jnp.zeros((1, tq), F32)
    acc0 = jnp.zeros((V_HEAD_DIM, tq), F32)
    _, l, acc = lax.fori_loop(0, n_kv, body, (m0, l0, acc0))
    o_ref[...] = (acc / l).astype(o_ref.dtype)


def _attn_call(q_t, k_all, v_t, tok_off, n_seq, seq_len, tq, tk):
    q_tiles = seq_len // tq
    tile0 = tok_off // tq
    seq0 = tok_off // seq_len
    return pl.pallas_call(
        functools.partial(_attn_kernel, tk=tk),
        grid=(n_seq, N_HEADS, q_tiles),
        in_specs=[
            pl.BlockSpec((1, HEAD_PAD, tq), lambda b, h, i: (h, 0, tile0 + b * q_tiles + i)),
            pl.BlockSpec((seq_len, HEAD_PAD), lambda b, h, i: (seq0 + b, h)),
            pl.BlockSpec((V_HEAD_DIM, seq_len), lambda b, h, i: (h, seq0 + b)),
        ],
        out_specs=pl.BlockSpec((V_HEAD_DIM, tq), lambda b, h, i: (h, b * q_tiles + i)),
        out_shape=jax.ShapeDtypeStruct((N_HEADS * V_HEAD_DIM, n_seq * seq_len), BF16),
        compiler_params=_cparams("parallel", "parallel", "parallel"),
        name="mla_attention",
    )(q_t, k_all, v_t)


def _conv_kernel(cur_ref, prev_ref, next_ref, w_ref, b_ref, lng_ref, lnb_ref, og_ref, o_ref,
                 ext_ref, *, seg_tiles, rows):
    i = pl.program_id(0)
    tc = cur_ref.shape[0]
    is_start = _tile_is_boundary(i, seg_tiles, 0)
    is_end = _tile_is_boundary(i, seg_tiles, 1)
    ext_ref[0:HALO, :] = jnp.where(is_start, 0.0, prev_ref[...])
    ext_ref[HALO:HALO + tc, :] = cur_ref[...]
    ext_ref[HALO + tc:2 * HALO + tc, :] = jnp.where(is_end, 0.0, next_ref[...])
    bias = b_ref[...]
    lng = lng_ref[...]
    lnb = lnb_ref[...]
    og = og_ref[...]
    for c in range(tc // rows):
        base = HALO - CONV_PAD + c * rows
        acc = jnp.broadcast_to(bias, (rows, D_CONV))
        for k in range(CONV_WIDTH):
            acc = acc + w_ref[k:k + 1, :] * ext_ref[base + k:base + k + rows, :]
        mu = jnp.mean(acc, axis=-1, keepdims=True)
        d = acc - mu
        var = jnp.mean(d * d, axis=-1, keepdims=True)
        y = d * lax.rsqrt(var + EPS) * lng + lnb
        y = y * jax.nn.sigmoid(y)
        o_ref[c * rows:(c + 1) * rows, :] = _rms(y, og).astype(o_ref.dtype)


def _tile_is_boundary(i, seg_tiles, end):
    out = None
    for first, n, per in reversed(seg_tiles):
        rel = (i - first + end) % per == 0
        out = rel if out is None else jnp.where(i < first + n, rel, out)
    return out


def _conv_call(hc, lw, segs, tc, rows):
    T = hc.shape[0]
    n_tiles = T // tc
    hb = tc // HALO
    last_halo = T // HALO - 1
    seg_tiles = tuple((off // tc, n * s // tc, s // tc) for off, n, s in segs)
    full = lambda shape: pl.BlockSpec(shape, lambda i: (0,) * len(shape))
    return pl.pallas_call(
        functools.partial(_conv_kernel, seg_tiles=seg_tiles, rows=rows),
        grid=(n_tiles,),
        in_specs=[
            pl.BlockSpec((tc, D_CONV), lambda i: (i, 0)),
            pl.BlockSpec((HALO, D_CONV), lambda i: (jnp.maximum(i * hb - 1, 0), 0)),
            pl.BlockSpec((HALO, D_CONV), lambda i: (jnp.minimum((i + 1) * hb, last_halo), 0)),
            full((CONV_WIDTH, D_CONV)),
            full((1, D_CONV)), full((1, D_CONV)), full((1, D_CONV)), full((1, D_CONV)),
        ],
        out_specs=pl.BlockSpec((tc, D_CONV), lambda i: (i, 0)),
        out_shape=jax.ShapeDtypeStruct((T, D_CONV), BF16),
        scratch_shapes=[pltpu.VMEM((tc + 2 * HALO, D_CONV), F32)],
        compiler_params=_cparams("parallel"),
        name="conv_branch",
    )(hc, hc, hc, lw["conv_dw_w"], lw["conv_dw_b"], lw["conv_ln_g"], lw["conv_ln_b"],
      lw["conv_out_norm_g"])


def _post_kernel(ot_ref, oc_ref, x_ref, ga_ref, wa_ref, wc_ref, y_ref):
    ot = ot_ref[...].astype(F32)
    r = lax.rsqrt(jnp.mean(ot * ot, axis=0, keepdims=True) + EPS)
    oa = (jnp.transpose(ot * r) * ga_ref[...]).astype(BF16)
    y_ref[...] = x_ref[...] + _dot(oa, wa_ref[...]) + _dot(oc_ref[...], wc_ref[...])


def _post_call(o_t, ocn, x, lw, tm):
    T = x.shape[0]
    full = lambda shape: pl.BlockSpec(shape, lambda i: (0,) * len(shape))
    return pl.pallas_call(
        _post_kernel,
        grid=(T // tm,),
        in_specs=[
            pl.BlockSpec((D_ATTN, tm), lambda i: (0, i)),
            pl.BlockSpec((tm, D_CONV), lambda i: (i, 0)),
            pl.BlockSpec((tm, D_MODEL), lambda i: (i, 0)),
            full((1, D_ATTN)),
            full((D_ATTN, D_MODEL)),
            full((D_CONV, D_MODEL)),
        ],
        out_specs=pl.BlockSpec((tm, D_MODEL), lambda i: (i, 0)),
        out_shape=jax.ShapeDtypeStruct((T, D_MODEL), F32),
        compiler_params=_cparams("parallel"),
        name="mixer_out",
    )(o_t, ocn, x, lw["attn_out_norm_g"], lw["w_out_a"], lw["w_out_c"])


def _dense_ffn_kernel(x_ref, g_ref, wg_ref, wu_ref, wd_ref, y_ref, h_ref, acc_ref):
    j = pl.program_id(1)

    @pl.when(j == 0)
    def _():
        x = x_ref[...]
        h_ref[...] = _rms(x, g_ref[...]).astype(BF16)
        acc_ref[...] = x

    h = h_ref[...]
    gate = _dot(h, wg_ref[...])
    up = _dot(h, wu_ref[...])
    a = (gate * jax.nn.sigmoid(gate) * up).astype(BF16)
    acc_ref[...] += _dot(a, wd_ref[...])

    @pl.when(j == pl.num_programs(1) - 1)
    def _():
        y_ref[...] = acc_ref[...]


def _dense_ffn_call(x, g, wg, wu, wd, tm, tf):
    T = x.shape[0]
    return pl.pallas_call(
        _dense_ffn_kernel,
        grid=(T // tm, D_FF // tf),
        in_specs=[
            pl.BlockSpec((tm, D_MODEL), lambda i, j: (i, 0)),
            pl.BlockSpec((1, D_MODEL), lambda i, j: (0, 0)),
            pl.BlockSpec((D_MODEL, tf), lambda i, j: (0, j)),
            pl.BlockSpec((D_MODEL, tf), lambda i, j: (0, j)),
            pl.BlockSpec((tf, D_MODEL), lambda i, j: (j, 0)),
        ],
        out_specs=pl.BlockSpec((tm, D_MODEL), lambda i, j: (i, 0)),
        out_shape=jax.ShapeDtypeStruct((T, D_MODEL), F32),
        scratch_shapes=[pltpu.VMEM((tm, D_MODEL), BF16), pltpu.VMEM((tm, D_MODEL), F32)],
        compiler_params=_cparams("parallel", "arbitrary"),
        name="dense_ffn",
    )(x, g, wg, wu, wd)


def _router_gates(x, h_f32, wr_hi, wr_lo, br):
    h_hi = h_f32.astype(BF16)
    h_lo = (h_f32 - h_hi.astype(F32)).astype(BF16)
    logits = _dot(h_hi, wr_hi) + _dot(h_lo, wr_hi) + _dot(h_hi, wr_lo) + br
    mx = jnp.max(logits, axis=-1, keepdims=True)
    e = jnp.exp(logits - mx)
    probs = e / jnp.sum(e, axis=-1, keepdims=True)
    lane = lax.broadcasted_iota(jnp.int32, probs.shape, 1)
    p1 = jnp.max(probs, axis=-1, keepdims=True)
    i1 = jnp.min(jnp.where(probs == p1, lane, LANES), axis=-1, keepdims=True)
    sel1 = lane == i1
    rest = jnp.where(sel1, -1.0, probs)
    p2 = jnp.max(rest, axis=-1, keepdims=True)
    i2 = jnp.min(jnp.where(rest == p2, lane, LANES), axis=-1, keepdims=True)
    sel2 = lane == i2
    denom = p1 + p2
    return jnp.where(sel1, p1 / denom, 0.0) + jnp.where(sel2, p2 / denom, 0.0)


def _moe_dense_kernel(x_ref, g_ref, wrh_ref, wrl_ref, br_ref, wg_ref, wu_ref, wd_ref, y_ref,
                      h_ref, gates_ref, acc_ref):
    e = pl.program_id(1)

    @pl.when(e == 0)
    def _():
        x = x_ref[...]
        hf = _rms(x, g_ref[...])
        h_ref[...] = hf.astype(BF16)
        gates_ref[...] = _router_gates(x, hf, wrh_ref[...], wrl_ref[...], br_ref[...])
        acc_ref[...] = x

    h = h_ref[...]
    gate = _dot(h, wg_ref[0])
    up = _dot(h, wu_ref[0])
    a = (gate * jax.nn.sigmoid(gate) * up).astype(BF16)
    gates = gates_ref[...]
    lane = lax.broadcasted_iota(jnp.int32, gates.shape, 1)
    ge = jnp.sum(jnp.where(lane == e, gates, 0.0), axis=-1, keepdims=True)
    acc_ref[...] += ge * _dot(a, wd_ref[0])

    @pl.when(e == pl.num_programs(1) - 1)
    def _():
        y_ref[...] = acc_ref[...]


def _moe_dense_call(x, g, wr_hi, wr_lo, br, wg, wu, wd, tm):
    T = x.shape[0]
    full2 = lambda shape: pl.BlockSpec(shape, lambda i, e: (0,) * len(shape))
    return pl.pallas_call(
        _moe_dense_kernel,
        grid=(T // tm, N_EXPERTS),
        in_specs=[
            pl.BlockSpec((tm, D_MODEL), lambda i, e: (i, 0)),
            full2((1, D_MODEL)),
            full2((D_MODEL, LANES)),
            full2((D_MODEL, LANES)),
            full2((1, LANES)),
            pl.BlockSpec((1, D_MODEL, D_EXPERT), lambda i, e: (e, 0, 0)),
            pl.BlockSpec((1, D_MODEL, D_EXPERT), lambda i, e: (e, 0, 0)),
            pl.BlockSpec((1, D_EXPERT, D_MODEL), lambda i, e: (e, 0, 0)),
        ],
        out_specs=pl.BlockSpec((tm, D_MODEL), lambda i, e: (i, 0)),
        out_shape=jax.ShapeDtypeStruct((T, D_MODEL), F32),
        scratch_shapes=[pltpu.VMEM((tm, D_MODEL), BF16), pltpu.VMEM((tm, LANES), F32),
                        pltpu.VMEM((tm, D_MODEL), F32)],
        compiler_params=_cparams("parallel", "arbitrary"),
        name="moe_ffn",
    )(x, g, wr_hi, wr_lo, br, wg, wu, wd)


def _final_norm_kernel(x_ref, g_ref, y_ref):
    y_ref[...] = _rms(x_ref[...], g_ref[...])


def _final_norm_call(x, g, tok_off, n_tok, tm):
    tile0 = tok_off // tm
    return pl.pallas_call(
        _final_norm_kernel,
        grid=(n_tok // tm,),
        in_specs=[pl.BlockSpec((tm, D_MODEL), lambda i: (tile0 + i, 0)),
                  pl.BlockSpec((1, D_MODEL), lambda i: (0, 0))],
        out_specs=pl.BlockSpec((tm, D_MODEL), lambda i: (i, 0)),
        out_shape=jax.ShapeDtypeStruct((n_tok, D_MODEL), F32),
        compiler_params=_cparams("parallel"),
        name="final_norm",
    )(x, g)


def _rope_tables(max_len):
    pos = jnp.arange(max_len, dtype=F32)
    freqs = ROPE_THETA ** (-(jnp.arange(0, QK_ROPE_DIM, 2, dtype=F32) / QK_ROPE_DIM))
    ang = pos[:, None] * freqs[None, :]
    ang = jnp.concatenate([ang, ang], axis=-1)
    cos, sin = jnp.cos(ang), jnp.sin(ang)
    pad = lambda t: jnp.pad(t, ((0, 0), (QK_NOPE_DIM, HEAD_PAD - QK_HEAD_DIM)))
    return {"cosk": pad(cos), "sink": pad(sin),
            "cosq_t": (cos * Q_SCALE).T, "sinq_t": (sin * Q_SCALE).T}


def _rot_cols(w):
    half = QK_ROPE_DIM // 2
    return jnp.concatenate([-w[..., half:], w[..., :half]], axis=-1)


def _layer_weights(i, p):
    row = lambda v: v.reshape(1, -1)
    w_in = p["w_in"][i]
    w_kr = w_in[:, OFF_KR:]
    place = lambda w: jnp.pad(w, ((0, 0), (QK_NOPE_DIM, HEAD_PAD - QK_HEAD_DIM)))
    w_in_ext = jnp.concatenate([w_in[:, :OFF_KR], place(w_kr), place(_rot_cols(w_kr))], axis=1)
    w_uq = p["w_uq"][i].reshape(Q_LORA_RANK, N_HEADS, QK_HEAD_DIM)
    wq_pad = jnp.pad(w_uq, ((0, 0), (0, 0), (0, HEAD_PAD - QK_HEAD_DIM)))
    wq_t = wq_pad.reshape(Q_LORA_RANK, N_HEADS * HEAD_PAD).T
    wqrot_t = _rot_cols(w_uq[..., QK_NOPE_DIM:]).reshape(Q_LORA_RANK, N_HEADS * QK_ROPE_DIM).T
    w_ukv = p["w_ukv"][i].reshape(KV_LORA_RANK, N_HEADS, QK_NOPE_DIM + V_HEAD_DIM)
    wk = jnp.pad(w_ukv[..., :QK_NOPE_DIM], ((0, 0), (0, 0), (0, HEAD_PAD - QK_NOPE_DIM)))
    wk = wk.reshape(KV_LORA_RANK, N_HEADS * HEAD_PAD)
    wv_t = w_ukv[..., QK_NOPE_DIM:].reshape(KV_LORA_RANK, N_HEADS * V_HEAD_DIM).T
    w_out = p["w_out"][i]
    return {
        "attn_norm_g": row(p["attn_norm_g"][i]),
        "w_in": w_in_ext.astype(BF16),
        "q_norm_g": row(p["q_norm_g"][i]),
        "wq_t": wq_t.astype(BF16),
        "wqrot_t": wqrot_t.astype(BF16),
        "kv_norm_g": row(p["kv_norm_g"][i]),
        "wk": wk.astype(BF16),
        "wv_t": wv_t.astype(BF16),
        "conv_dw_w": p["conv_dw_w"][i].reshape(CONV_WIDTH, D_CONV),
        "conv_dw_b": row(p["conv_dw_b"][i]),
        "conv_ln_g": row(p["conv_ln_g"][i]),
        "conv_ln_b": row(p["conv_ln_b"][i]),
        "conv_out_norm_g": row(p["conv_out_norm_g"][i]),
        "attn_out_norm_g": row(p["attn_out_norm_g"][i]),
        "w_out_a": w_out[:D_ATTN].astype(BF16),
        "w_out_c": w_out[D_ATTN:].astype(BF16),
        "ffn_norm_g": row(p["ffn_norm_g"][i]),
    }


def _split_hi_lo(w):
    hi = w.astype(BF16)
    lo = (w - hi.astype(F32)).astype(BF16)
    return hi, lo


def _tile_sizes(segs):
    s_min = min(s for _, _, s in segs)
    return {
        "tm": min(512, s_min),
        "tc": min(512, s_min),
        "rows": 64,
        "tq": min(512, s_min),
        "tk": min(256, s_min),
        "tf": D_FF // 2,
    }


def kernel(x_prompt, x_sample, attn_norm_g, w_in, q_norm_g, w_uq, kv_norm_g, w_ukv, conv_dw_w,
           conv_dw_b, conv_ln_g, conv_ln_b, attn_out_norm_g, conv_out_norm_g, w_out, ffn_norm_g,
           dense_w_gate, dense_w_up, dense_w_down, moe_w_router, moe_b_router, moe_w_gate,
           moe_w_up, moe_w_down, final_norm_g):
    p = dict(attn_norm_g=attn_norm_g, w_in=w_in, q_norm_g=q_norm_g, w_uq=w_uq,
             kv_norm_g=kv_norm_g, w_ukv=w_ukv, conv_dw_w=conv_dw_w, conv_dw_b=conv_dw_b,
             conv_ln_g=conv_ln_g, conv_ln_b=conv_ln_b, attn_out_norm_g=attn_out_norm_g,
             conv_out_norm_g=conv_out_norm_g, w_out=w_out, ffn_norm_g=ffn_norm_g)
    bp, sp, _ = x_prompt.shape
    bs, ss, _ = x_sample.shape
    n_p, n_s = bp * sp, bs * ss
    segs = ((0, bp, sp), (n_p, bs, ss))
    assert n_p % ss == 0 and n_p % sp == 0
    ts = _tile_sizes(segs)
    tabs = _rope_tables(max(sp, ss))
    x = jnp.concatenate([x_prompt.reshape(n_p, D_MODEL), x_sample.reshape(n_s, D_MODEL)], axis=0)

    for i in range(DEPTH):
        lw = _layer_weights(i, p)
        hc, q_t, k_all, v_t = _pre_call(x, lw, tabs, segs, ts["tm"])
        o_t = jnp.concatenate(
            [_attn_call(q_t, k_all, v_t, off, n, s, ts["tq"], ts["tk"]) for off, n, s in segs],
            axis=1)
        ocn = _conv_call(hc, lw, segs, ts["tc"], ts["rows"])
        x = _post_call(o_t, ocn, x, lw, ts["tm"])
        j = i // 2
        if i % 2 == 0:
            x = _dense_ffn_call(x, lw["ffn_norm_g"], dense_w_gate[j].astype(BF16),
                                dense_w_up[j].astype(BF16), dense_w_down[j].astype(BF16),
                                ts["tm"], ts["tf"])
        else:
            wr = jnp.pad(moe_w_router[j], ((0, 0), (0, LANES - N_EXPERTS)))
            wr_hi, wr_lo = _split_hi_lo(wr)
            br = jnp.pad(moe_b_router[j], (0, LANES - N_EXPERTS),
                         constant_values=NEG_BIG).reshape(1, LANES)
            x = _moe_dense_call(x, lw["ffn_norm_g"], wr_hi, wr_lo, br,
                                moe_w_gate[j].astype(BF16), moe_w_up[j].astype(BF16),
                                moe_w_down[j].astype(BF16), ts["tm"])

    g = final_norm_g.reshape(1, D_MODEL)
    y_p = _final_norm_call(x, g, 0, n_p, ts["tm"]).reshape(bp, sp, D_MODEL)
    y_s = _final_norm_call(x, g, n_p, n_s, ts["tm"]).reshape(bs, ss, D_MODEL)
    return (y_p, y_s)
```

```python
import functools
import math

import jax
import jax.numpy as jnp
import numpy as np
from jax import lax
from jax.experimental import pallas as pl
from jax.experimental.pallas import tpu as pltpu

D_MODEL = 1024
DEPTH = 4
D_ATTN = 512
D_CONV = 512
N_HEADS = 8
QK_NOPE_DIM = 64
QK_ROPE_DIM = 32
V_HEAD_DIM = 64
QK_HEAD_DIM = QK_NOPE_DIM + QK_ROPE_DIM
Q_LORA_RANK = 256
KV_LORA_RANK = 128
ROPE_THETA = 10000.0
CONV_WIDTH = 31
CONV_PAD = (CONV_WIDTH - 1) // 2
D_FF = 2816
N_EXPERTS = 8
D_EXPERT = 1024
EPS = 1e-6
OFF_CQ = 2 * D_CONV
OFF_CKV = OFF_CQ + Q_LORA_RANK
OFF_KR = OFF_CKV + KV_LORA_RANK

LANES = 128
HEAD_PAD = LANES
HALO = 16
BF16_SUBLANES = 16
V_ROWS = V_HEAD_DIM + BF16_SUBLANES
VMEM_LIMIT_BYTES = 56 * 1024 * 1024

BF16 = jnp.bfloat16
F32 = jnp.float32
NEG_BIG = -1e30
Q_SCALE = (QK_HEAD_DIM ** -0.5) * math.log2(math.e)


def _cparams(*sem):
    return pltpu.CompilerParams(dimension_semantics=sem, vmem_limit_bytes=VMEM_LIMIT_BYTES)


def _rms(x, g):
    return x * lax.rsqrt(jnp.mean(x * x, axis=-1, keepdims=True) + EPS) * g


def _dot(a, b):
    return jnp.dot(a, b, preferred_element_type=F32)


def _dot_nt(a, b):
    return lax.dot_general(a, b, (((1,), (1,)), ((), ())), preferred_element_type=F32)


def _pre_kernel(x_ref, g_ref, win_ref, qg_ref, wqt_ref, wqrt_ref, kvg_ref, wk_ref, wvt_ref,
                cosk_ref, sink_ref, cosq_ref, sinq_ref,
                hc_ref, q_ref, k_ref, vt_ref):
    x = x_ref[...]
    h = _rms(x, g_ref[...]).astype(BF16)
    z = _dot(h, win_ref[...])
    hc_ref[...] = z[:, :D_CONV] * jax.nn.sigmoid(z[:, D_CONV:2 * D_CONV])
    cqn = _rms(z[:, OFF_CQ:OFF_CKV], qg_ref[...]).astype(BF16)
    ckvn = _rms(z[:, OFF_CKV:OFF_KR], kvg_ref[...]).astype(BF16)
    kr = (z[:, OFF_KR:OFF_KR + LANES] * cosk_ref[...]
          + z[:, OFF_KR + LANES:OFF_KR + 2 * LANES] * sink_ref[...])
    kall = _dot(ckvn, wk_ref[...])
    for hd in range(N_HEADS):
        sl = slice(hd * HEAD_PAD, (hd + 1) * HEAD_PAD)
        k_ref[:, sl] = (kall[:, sl] + kr).astype(BF16)
    vt = _dot_nt(wvt_ref[...], ckvn)
    ones = jnp.ones((V_ROWS - V_HEAD_DIM, x.shape[0]), BF16)
    for hd in range(N_HEADS):
        vt_ref[hd, 0:V_HEAD_DIM, :] = vt[hd * V_HEAD_DIM:(hd + 1) * V_HEAD_DIM].astype(BF16)
        vt_ref[hd, V_HEAD_DIM:V_ROWS, :] = ones
    qt = _dot_nt(wqt_ref[...], cqn)
    qrt = _dot_nt(wqrt_ref[...], cqn)
    cq = cosq_ref[...]
    sq = sinq_ref[...]
    zeros = jnp.zeros((HEAD_PAD - QK_HEAD_DIM, x.shape[0]), BF16)
    for hd in range(N_HEADS):
        base = hd * HEAD_PAD
        q_ref[hd, 0:QK_NOPE_DIM, :] = (qt[base:base + QK_NOPE_DIM] * Q_SCALE).astype(BF16)
        rope = (qt[base + QK_NOPE_DIM:base + QK_HEAD_DIM] * cq
                + qrt[hd * QK_ROPE_DIM:(hd + 1) * QK_ROPE_DIM] * sq)
        q_ref[hd, QK_NOPE_DIM:QK_HEAD_DIM, :] = rope.astype(BF16)
        q_ref[hd, QK_HEAD_DIM:HEAD_PAD, :] = zeros


def _pre_call(x, lw, tabs, segs, tm):
    T = x.shape[0]
    n_tiles = T // tm
    pos_map = _pos_block_map(segs, tm)
    full = lambda shape: pl.BlockSpec(shape, lambda i: (0,) * len(shape))
    in_cols = lw["w_in"].shape[1]
    return pl.pallas_call(
        _pre_kernel,
        grid=(n_tiles,),
        in_specs=[
            pl.BlockSpec((tm, D_MODEL), lambda i: (i, 0)),
            full((1, D_MODEL)),
            full((D_MODEL, in_cols)),
            full((1, Q_LORA_RANK)),
            full((N_HEADS * HEAD_PAD, Q_LORA_RANK)),
            full((N_HEADS * QK_ROPE_DIM, Q_LORA_RANK)),
            full((1, KV_LORA_RANK)),
            full((KV_LORA_RANK, N_HEADS * HEAD_PAD)),
            full((N_HEADS * V_HEAD_DIM, KV_LORA_RANK)),
            pl.BlockSpec((tm, LANES), lambda i: (pos_map(i), 0)),
            pl.BlockSpec((tm, LANES), lambda i: (pos_map(i), 0)),
            pl.BlockSpec((QK_ROPE_DIM, tm), lambda i: (0, pos_map(i))),
            pl.BlockSpec((QK_ROPE_DIM, tm), lambda i: (0, pos_map(i))),
        ],
        out_specs=[
            pl.BlockSpec((tm, D_CONV), lambda i: (i, 0)),
            pl.BlockSpec((N_HEADS, HEAD_PAD, tm), lambda i: (0, 0, i)),
            pl.BlockSpec((tm, N_HEADS * HEAD_PAD), lambda i: (i, 0)),
            pl.BlockSpec((N_HEADS, V_ROWS, tm), lambda i: (0, 0, i)),
        ],
        out_shape=[
            jax.ShapeDtypeStruct((T, D_CONV), F32),
            jax.ShapeDtypeStruct((N_HEADS, HEAD_PAD, T), BF16),
            jax.ShapeDtypeStruct((T, N_HEADS * HEAD_PAD), BF16),
            jax.ShapeDtypeStruct((N_HEADS, V_ROWS, T), BF16),
        ],
        compiler_params=_cparams("parallel"),
        name="layer_front",
    )(x, lw["attn_norm_g"], lw["w_in"], lw["q_norm_g"], lw["wq_t"], lw["wqrot_t"],
      lw["kv_norm_g"], lw["wk"], lw["wv_t"], tabs["cosk"], tabs["sink"], tabs["cosq_t"],
      tabs["sinq_t"])


def _pos_block_map(segs, tile):
    def pos_map(i):
        out = None
        for off, n_seq, seq_len in reversed(segs):
            blk = (i - off // tile) % (seq_len // tile)
            out = blk if out is None else jnp.where(i < (off + n_seq * seq_len) // tile, blk, out)
        return out
    return pos_map


def _attn_kernel(q_ref, k_ref, vt_ref, o_ref, s_ref, p_ref, cm_ref, *, tk, nb):
    q = q_ref[0]
    tq = q.shape[1]
    n_kv = k_ref.shape[0] // tk
    n_body = n_kv // nb
    look = nb - 1

    def stage_a(j, slot):
        start = pl.multiple_of(j * tk, tk)
        s = _dot(k_ref[pl.ds(start, tk), :], q)
        s_ref[slot] = s
        cm_ref[slot] = jnp.max(s, axis=0, keepdims=True)

    def stage_b(m, slot):
        m_new = jnp.maximum(m, cm_ref[slot])
        alpha = jnp.exp2(m - m_new)
        p_ref[slot] = jnp.exp2(s_ref[slot] - m_new).astype(BF16)
        return m_new, alpha

    def stage_c(j, slot, acc, alpha):
        start = pl.multiple_of(j * tk, tk)
        return alpha * acc + _dot(vt_ref[0, :, pl.ds(start, tk)], p_ref[slot])

    def body(j0, carry, first=False, last=False):
        m, acc, alpha = carry
        for i in range(nb):
            if not (last and i >= 1):
                stage_a(j0 + i + look, (i + look) % nb)
            if not (first and i == 0):
                acc = stage_c(j0 + i - 1, (i - 1) % nb, acc, alpha)
            m, alpha = stage_b(m, i)
        return m, acc, alpha

    for j in range(look):
        stage_a(j, j)
    carry = (jnp.full((1, tq), NEG_BIG, F32), jnp.zeros((V_ROWS, tq), F32),
             jnp.ones((1, tq), F32))
    carry = body(0, carry, first=True, last=(n_body == 1))
    if n_body > 2:
        carry = lax.fori_loop(1, n_body - 1, lambda jb, c: body(jb * nb, c), carry)
    if n_body > 1:
        carry = body((n_body - 1) * nb, carry, last=True)
    _, acc, alpha = carry
    acc = stage_c(n_kv - 1, (n_kv - 1) % nb, acc, alpha)
    o_ref[...] = (acc[:V_HEAD_DIM] / acc[V_HEAD_DIM:V_HEAD_DIM + 1]).astype(o_ref.dtype)


def _attn_call(q_t, k_all, v_t, tok_off, n_seq, seq_len, tq, tk, nb):
    assert (seq_len // tk) % nb == 0
    q_tiles = seq_len // tq
    tile0 = tok_off // tq
    seq0 = tok_off // seq_len
    return pl.pallas_call(
        functools.partial(_attn_kernel, tk=tk, nb=nb),
        grid=(n_seq, N_HEADS, q_tiles),
        in_specs=[
            pl.BlockSpec((1, HEAD_PAD, tq), lambda b, h, i: (h, 0, tile0 + b * q_tiles + i)),
            pl.BlockSpec((seq_len, HEAD_PAD), lambda b, h, i: (seq0 + b, h)),
            pl.BlockSpec((1, V_ROWS, seq_len), lambda b, h, i: (h, 0, seq0 + b)),
        ],
        out_specs=pl.BlockSpec((V_HEAD_DIM, tq), lambda b, h, i: (h, b * q_tiles + i)),
        out_shape=jax.ShapeDtypeStruct((N_HEADS * V_HEAD_DIM, n_seq * seq_len), BF16),
        scratch_shapes=[pltpu.VMEM((nb, tk, tq), F32), pltpu.VMEM((nb, tk, tq), BF16),
                        pltpu.VMEM((nb, 1, tq), F32)],
        compiler_params=_cparams("parallel", "parallel", "parallel"),
        name="mla_attention",
    )(q_t, k_all, v_t)


def _conv_kernel(cur_ref, prev_ref, next_ref, w_ref, b_ref, lng_ref, lnb_ref, og_ref, o_ref,
                 ext_ref, *, seg_tiles, rows):
    i = pl.program_id(0)
    tc = cur_ref.shape[0]
    is_start = _tile_is_boundary(i, seg_tiles, 0)
    is_end = _tile_is_boundary(i, seg_tiles, 1)
    ext_ref[0:HALO, :] = jnp.where(is_start, 0.0, prev_ref[...])
    ext_ref[HALO:HALO + tc, :] = cur_ref[...]
    ext_ref[HALO + tc:2 * HALO + tc, :] = jnp.where(is_end, 0.0, next_ref[...])
    bias = b_ref[...]
    lng = lng_ref[...]
    lnb = lnb_ref[...]
    og = og_ref[...]
    for c in range(tc // rows):
        base = HALO - CONV_PAD + c * rows
        acc = jnp.broadcast_to(bias, (rows, D_CONV))
        for k in range(CONV_WIDTH):
            acc = acc + w_ref[k:k + 1, :] * ext_ref[base + k:base + k + rows, :]
        mu = jnp.mean(acc, axis=-1, keepdims=True)
        d = acc - mu
        var = jnp.mean(d * d, axis=-1, keepdims=True)
        y = d * lax.rsqrt(var + EPS) * lng + lnb
        y = y * jax.nn.sigmoid(y)
        o_ref[c * rows:(c + 1) * rows, :] = _rms(y, og).astype(o_ref.dtype)


def _tile_is_boundary(i, seg_tiles, end):
    out = None
    for first, n, per in reversed(seg_tiles):
        rel = (i - first + end) % per == 0
        out = rel if out is None else jnp.where(i < first + n, rel, out)
    return out


def _conv_call(hc, lw, segs, tc, rows):
    T = hc.shape[0]
    n_tiles = T // tc
    hb = tc // HALO
    last_halo = T // HALO - 1
    seg_tiles = tuple((off // tc, n * s // tc, s // tc) for off, n, s in segs)
    full = lambda shape: pl.BlockSpec(shape, lambda i: (0,) * len(shape))
    return pl.pallas_call(
        functools.partial(_conv_kernel, seg_tiles=seg_tiles, rows=rows),
        grid=(n_tiles,),
        in_specs=[
            pl.BlockSpec((tc, D_CONV), lambda i: (i, 0)),
            pl.BlockSpec((HALO, D_CONV), lambda i: (jnp.maximum(i * hb - 1, 0), 0)),
            pl.BlockSpec((HALO, D_CONV), lambda i: (jnp.minimum((i + 1) * hb, last_halo), 0)),
            full((CONV_WIDTH, D_CONV)),
            full((1, D_CONV)), full((1, D_CONV)), full((1, D_CONV)), full((1, D_CONV)),
        ],
        out_specs=pl.BlockSpec((tc, D_CONV), lambda i: (i, 0)),
        out_shape=jax.ShapeDtypeStruct((T, D_CONV), BF16),
        scratch_shapes=[pltpu.VMEM((tc + 2 * HALO, D_CONV), F32)],
        compiler_params=_cparams("parallel"),
        name="conv_branch",
    )(hc, hc, hc, lw["conv_dw_w"], lw["conv_dw_b"], lw["conv_ln_g"], lw["conv_ln_b"],
      lw["conv_out_norm_g"])


def _post_kernel(ot_ref, oc_ref, x_ref, ga_ref, wa_ref, wc_ref, y_ref):
    ot = ot_ref[...].astype(F32)
    r = lax.rsqrt(jnp.mean(ot * ot, axis=0, keepdims=True) + EPS)
    oa = (jnp.transpose(ot * r) * ga_ref[...]).astype(BF16)
    y_ref[...] = x_ref[...] + _dot(oa, wa_ref[...]) + _dot(oc_ref[...], wc_ref[...])


def _post_call(o_t, ocn, x, lw, tm):
    T = x.shape[0]
    full = lambda shape: pl.BlockSpec(shape, lambda i: (0,) * len(shape))
    return pl.pallas_call(
        _post_kernel,
        grid=(T // tm,),
        in_specs=[
            pl.BlockSpec((D_ATTN, tm), lambda i: (0, i)),
            pl.BlockSpec((tm, D_CONV), lambda i: (i, 0)),
            pl.BlockSpec((tm, D_MODEL), lambda i: (i, 0)),
            full((1, D_ATTN)),
            full((D_ATTN, D_MODEL)),
            full((D_CONV, D_MODEL)),
        ],
        out_specs=pl.BlockSpec((tm, D_MODEL), lambda i: (i, 0)),
        out_shape=jax.ShapeDtypeStruct((T, D_MODEL), F32),
        compiler_params=_cparams("parallel"),
        name="mixer_out",
    )(o_t, ocn, x, lw["attn_out_norm_g"], lw["w_out_a"], lw["w_out_c"])


def _dense_ffn_kernel(x_ref, g_ref, wg_ref, wu_ref, wd_ref, y_ref, h_ref, acc_ref):
    j = pl.program_id(1)

    @pl.when(j == 0)
    def _():
        x = x_ref[...]
        h_ref[...] = _rms(x, g_ref[...]).astype(BF16)
        acc_ref[...] = x

    h = h_ref[...]
    gate = _dot(h, wg_ref[...])
    up = _dot(h, wu_ref[...])
    a = (gate * jax.nn.sigmoid(gate) * up).astype(BF16)
    acc_ref[...] += _dot(a, wd_ref[...])

    @pl.when(j == pl.num_programs(1) - 1)
    def _():
        y_ref[...] = acc_ref[...]


def _dense_ffn_call(x, g, wg, wu, wd, tm, tf):
    T = x.shape[0]
    return pl.pallas_call(
        _dense_ffn_kernel,
        grid=(T // tm, D_FF // tf),
        in_specs=[
            pl.BlockSpec((tm, D_MODEL), lambda i, j: (i, 0)),
            pl.BlockSpec((1, D_MODEL), lambda i, j: (0, 0)),
            pl.BlockSpec((D_MODEL, tf), lambda i, j: (0, j)),
            pl.BlockSpec((D_MODEL, tf), lambda i, j: (0, j)),
            pl.BlockSpec((tf, D_MODEL), lambda i, j: (j, 0)),
        ],
        out_specs=pl.BlockSpec((tm, D_MODEL), lambda i, j: (i, 0)),
        out_shape=jax.ShapeDtypeStruct((T, D_MODEL), F32),
        scratch_shapes=[pltpu.VMEM((tm, D_MODEL), BF16), pltpu.VMEM((tm, D_MODEL), F32)],
        compiler_params=_cparams("parallel", "arbitrary"),
        name="dense_ffn",
    )(x, g, wg, wu, wd)


def _router_gates(x, h_f32, wr_hi, wr_lo, br):
    h_hi = h_f32.astype(BF16)
    h_lo = (h_f32 - h_hi.astype(F32)).astype(BF16)
    logits = _dot(h_hi, wr_hi) + _dot(h_lo, wr_hi) + _dot(h_hi, wr_lo) + br
    mx = jnp.max(logits, axis=-1, keepdims=True)
    e = jnp.exp(logits - mx)
    probs = e / jnp.sum(e, axis=-1, keepdims=True)
    lane = lax.broadcasted_iota(jnp.int32, probs.shape, 1)
    p1 = jnp.max(probs, axis=-1, keepdims=True)
    i1 = jnp.min(jnp.where(probs == p1, lane, LANES), axis=-1, keepdims=True)
    sel1 = lane == i1
    rest = jnp.where(sel1, -1.0, probs)
    p2 = jnp.max(rest, axis=-1, keepdims=True)
    i2 = jnp.min(jnp.where(rest == p2, lane, LANES), axis=-1, keepdims=True)
    sel2 = lane == i2
    denom = p1 + p2
    return jnp.where(sel1, p1 / denom, 0.0) + jnp.where(sel2, p2 / denom, 0.0)


def _moe_dense_kernel(x_ref, g_ref, wrh_ref, wrl_ref, br_ref, wg_ref, wu_ref, wd_ref, y_ref,
                      h_ref, gates_ref, acc_ref):
    e = pl.program_id(1)

    @pl.when(e == 0)
    def _():
        x = x_ref[...]
        hf = _rms(x, g_ref[...])
        h_ref[...] = hf.astype(BF16)
        gates_ref[...] = _router_gates(x, hf, wrh_ref[...], wrl_ref[...], br_ref[...])
        acc_ref[...] = x

    h = h_ref[...]
    gate = _dot(h, wg_ref[0])
    up = _dot(h, wu_ref[0])
    a = (gate * jax.nn.sigmoid(gate) * up).astype(BF16)
    gates = gates_ref[...]
    lane = lax.broadcasted_iota(jnp.int32, gates.shape, 1)
    ge = jnp.sum(jnp.where(lane == e, gates, 0.0), axis=-1, keepdims=True)
    acc_ref[...] += ge * _dot(a, wd_ref[0])

    @pl.when(e == pl.num_programs(1) - 1)
    def _():
        y_ref[...] = acc_ref[...]


def _moe_dense_call(x, g, wr_hi, wr_lo, br, wg, wu, wd, tm):
    T = x.shape[0]
    full2 = lambda shape: pl.BlockSpec(shape, lambda i, e: (0,) * len(shape))
    return pl.pallas_call(
        _moe_dense_kernel,
        grid=(T // tm, N_EXPERTS),
        in_specs=[
            pl.BlockSpec((tm, D_MODEL), lambda i, e: (i, 0)),
            full2((1, D_MODEL)),
            full2((D_MODEL, LANES)),
            full2((D_MODEL, LANES)),
            full2((1, LANES)),
            pl.BlockSpec((1, D_MODEL, D_EXPERT), lambda i, e: (e, 0, 0)),
            pl.BlockSpec((1, D_MODEL, D_EXPERT), lambda i, e: (e, 0, 0)),
            pl.BlockSpec((1, D_EXPERT, D_MODEL), lambda i, e: (e, 0, 0)),
        ],
        out_specs=pl.BlockSpec((tm, D_MODEL), lambda i, e: (i, 0)),
        out_shape=jax.ShapeDtypeStruct((T, D_MODEL), F32),
        scratch_shapes=[pltpu.VMEM((tm, D_MODEL), BF16), pltpu.VMEM((tm, LANES), F32),
                        pltpu.VMEM((tm, D_MODEL), F32)],
        compiler_params=_cparams("parallel", "arbitrary"),
        name="moe_ffn",
    )(x, g, wr_hi, wr_lo, br, wg, wu, wd)


def _final_norm_kernel(x_ref, g_ref, y_ref):
    y_ref[...] = _rms(x_ref[...], g_ref[...])


def _final_norm_call(x, g, tok_off, n_tok, tm):
    tile0 = tok_off // tm
    return pl.pallas_call(
        _final_norm_kernel,
        grid=(n_tok // tm,),
        in_specs=[pl.BlockSpec((tm, D_MODEL), lambda i: (tile0 + i, 0)),
                  pl.BlockSpec((1, D_MODEL), lambda i: (0, 0))],
        out_specs=pl.BlockSpec((tm, D_MODEL), lambda i: (i, 0)),
        out_shape=jax.ShapeDtypeStruct((n_tok, D_MODEL), F32),
        compiler_params=_cparams("parallel"),
        name="final_norm",
    )(x, g)


def _rope_tables(max_len):
    pos = jnp.arange(max_len, dtype=F32)
    freqs = ROPE_THETA ** (-(jnp.arange(0, QK_ROPE_DIM, 2, dtype=F32) / QK_ROPE_DIM))
    ang = pos[:, None] * freqs[None, :]
    ang = jnp.concatenate([ang, ang], axis=-1)
    cos, sin = jnp.cos(ang), jnp.sin(ang)
    pad = lambda t: jnp.pad(t, ((0, 0), (QK_NOPE_DIM, HEAD_PAD - QK_HEAD_DIM)))
    return {"cosk": pad(cos), "sink": pad(sin),
            "cosq_t": (cos * Q_SCALE).T, "sinq_t": (sin * Q_SCALE).T}


def _rot_cols(w):
    half = QK_ROPE_DIM // 2
    return jnp.concatenate([-w[..., half:], w[..., :half]], axis=-1)


def _layer_weights(i, p):
    row = lambda v: v.reshape(1, -1)
    w_in = p["w_in"][i]
    w_kr = w_in[:, OFF_KR:]
    place = lambda w: jnp.pad(w, ((0, 0), (QK_NOPE_DIM, HEAD_PAD - QK_HEAD_DIM)))
    w_in_ext = jnp.concatenate([w_in[:, :OFF_KR], place(w_kr), place(_rot_cols(w_kr))], axis=1)
    w_uq = p["w_uq"][i].reshape(Q_LORA_RANK, N_HEADS, QK_HEAD_DIM)
    wq_pad = jnp.pad(w_uq, ((0, 0), (0, 0), (0, HEAD_PAD - QK_HEAD_DIM)))
    wq_t = wq_pad.reshape(Q_LORA_RANK, N_HEADS * HEAD_PAD).T
    wqrot_t = _rot_cols(w_uq[..., QK_NOPE_DIM:]).reshape(Q_LORA_RANK, N_HEADS * QK_ROPE_DIM).T
    w_ukv = p["w_ukv"][i].reshape(KV_LORA_RANK, N_HEADS, QK_NOPE_DIM + V_HEAD_DIM)
    wk = jnp.pad(w_ukv[..., :QK_NOPE_DIM], ((0, 0), (0, 0), (0, HEAD_PAD - QK_NOPE_DIM)))
    wk = wk.reshape(KV_LORA_RANK, N_HEADS * HEAD_PAD)
    wv_t = w_ukv[..., QK_NOPE_DIM:].reshape(KV_LORA_RANK, N_HEADS * V_HEAD_DIM).T
    w_out = p["w_out"][i]
    return {
        "attn_norm_g": row(p["attn_norm_g"][i]),
        "w_in": w_in_ext.astype(BF16),
        "q_norm_g": row(p["q_norm_g"][i]),
        "wq_t": wq_t.astype(BF16),
        "wqrot_t": wqrot_t.astype(BF16),
        "kv_norm_g": row(p["kv_norm_g"][i]),
        "wk": wk.astype(BF16),
        "wv_t": wv_t.astype(BF16),
        "conv_dw_w": p["conv_dw_w"][i].reshape(CONV_WIDTH, D_CONV),
        "conv_dw_b": row(p["conv_dw_b"][i]),
        "conv_ln_g": row(p["conv_ln_g"][i]),
        "conv_ln_b": row(p["conv_ln_b"][i]),
        "conv_out_norm_g": row(p["conv_out_norm_g"][i]),
        "attn_out_norm_g": row(p["attn_out_norm_g"][i]),
        "w_out_a": w_out[:D_ATTN].astype(BF16),
        "w_out_c": w_out[D_ATTN:].astype(BF16),
        "ffn_norm_g": row(p["ffn_norm_g"][i]),
    }


def _split_hi_lo(w):
    hi = w.astype(BF16)
    lo = (w - hi.astype(F32)).astype(BF16)
    return hi, lo


def _tile_sizes(segs):
    s_min = min(s for _, _, s in segs)
    return {
        "tm": min(512, s_min),
        "tc": min(512, s_min),
        "rows": 64,
        "tq": min(512, s_min),
        "tk": min(512, s_min),
        "nb": 4,
        "tf": D_FF // 2,
    }


def kernel(x_prompt, x_sample, attn_norm_g, w_in, q_norm_g, w_uq, kv_norm_g, w_ukv, conv_dw_w,
           conv_dw_b, conv_ln_g, conv_ln_b, attn_out_norm_g, conv_out_norm_g, w_out, ffn_norm_g,
           dense_w_gate, dense_w_up, dense_w_down, moe_w_router, moe_b_router, moe_w_gate,
           moe_w_up, moe_w_down, final_norm_g):
    p = dict(attn_norm_g=attn_norm_g, w_in=w_in, q_norm_g=q_norm_g, w_uq=w_uq,
             kv_norm_g=kv_norm_g, w_ukv=w_ukv, conv_dw_w=conv_dw_w, conv_dw_b=conv_dw_b,
             conv_ln_g=conv_ln_g, conv_ln_b=conv_ln_b, attn_out_norm_g=attn_out_norm_g,
             conv_out_norm_g=conv_out_norm_g, w_out=w_out, ffn_norm_g=ffn_norm_g)
    bp, sp, _ = x_prompt.shape
    bs, ss, _ = x_sample.shape
    n_p, n_s = bp * sp, bs * ss
    segs = ((0, bp, sp), (n_p, bs, ss))
    assert n_p % ss == 0 and n_p % sp == 0
    ts = _tile_sizes(segs)
    tabs = _rope_tables(max(sp, ss))
    x = jnp.concatenate([x_prompt.reshape(n_p, D_MODEL), x_sample.reshape(n_s, D_MODEL)], axis=0)

    for i in range(DEPTH):
        lw = _layer_weights(i, p)
        hc, q_t, k_all, v_t = _pre_call(x, lw, tabs, segs, ts["tm"])
        o_t = jnp.concatenate(
            [_attn_call(q_t, k_all, v_t, off, n, s, ts["tq"], ts["tk"],
                        math.gcd(ts["nb"], s // ts["tk"])) for off, n, s in segs],
            axis=1)
        ocn = _conv_call(hc, lw, segs, ts["tc"], ts["rows"])
        x = _post_call(o_t, ocn, x, lw, ts["tm"])
        j = i // 2
        if i % 2 == 0:
            x = _dense_ffn_call(x, lw["ffn_norm_g"], dense_w_gate[j].astype(BF16),
                                dense_w_up[j].astype(BF16), dense_w_down[j].astype(BF16),
                                ts["tm"], ts["tf"])
        else:
            wr = jnp.pad(moe_w_router[j], ((0, 0), (0, LANES - N_EXPERTS)))
            wr_hi, wr_lo = _split_hi_lo(wr)
            br = jnp.pad(moe_b_router[j], (0, LANES - N_EXPERTS),
                         constant_values=NEG_BIG).reshape(1, LANES)
            x = _moe_dense_call(x, lw["ffn_norm_g"], wr_hi, wr_lo, br,
                                moe_w_gate[j].astype(BF16), moe_w_up[j].astype(BF16),
                                moe_w_down[j].astype(BF16), ts["tm"])

    g = final_norm_g.reshape(1, D_MODEL)
    y_p = _final_norm_call(x, g, 0, n_p, ts["tm"]).reshape(bp, sp, D_MODEL)
    y_s = _final_norm_call(x, g, n_p, n_s, ts["tm"]).reshape(bs, ss, D_MODEL)
    return (y_p, y_s)
```

```python
import functools
import math

import jax
import jax.numpy as jnp
import numpy as np
from jax import lax
from jax.experimental import pallas as pl
from jax.experimental.pallas import tpu as pltpu

D_MODEL = 1024
DEPTH = 4
D_ATTN = 512
D_CONV = 512
N_HEADS = 8
QK_NOPE_DIM = 64
QK_ROPE_DIM = 32
V_HEAD_DIM = 64
QK_HEAD_DIM = QK_NOPE_DIM + QK_ROPE_DIM
Q_LORA_RANK = 256
KV_LORA_RANK = 128
ROPE_THETA = 10000.0
CONV_WIDTH = 31
CONV_PAD = (CONV_WIDTH - 1) // 2
D_FF = 2816
N_EXPERTS = 8
D_EXPERT = 1024
EPS = 1e-6
OFF_CQ = 2 * D_CONV
OFF_CKV = OFF_CQ + Q_LORA_RANK
OFF_KR = OFF_CKV + KV_LORA_RANK

LANES = 128
HEAD_PAD = LANES
HALO = 16
BF16_SUBLANES = 16
V_ROWS = V_HEAD_DIM + BF16_SUBLANES
VMEM_LIMIT_BYTES = 56 * 1024 * 1024

BF16 = jnp.bfloat16
F32 = jnp.float32
NEG_BIG = -1e30
Q_SCALE = (QK_HEAD_DIM ** -0.5) * math.log2(math.e)


def _cparams(*sem):
    return pltpu.CompilerParams(dimension_semantics=sem, vmem_limit_bytes=VMEM_LIMIT_BYTES)


def _rms(x, g):
    return x * lax.rsqrt(jnp.mean(x * x, axis=-1, keepdims=True) + EPS) * g


def _dot(a, b):
    return jnp.dot(a, b, preferred_element_type=F32)


def _dot_nt(a, b):
    return lax.dot_general(a, b, (((1,), (1,)), ((), ())), preferred_element_type=F32)


def _pre_kernel(x_ref, g_ref, win_ref, qg_ref, wqt_ref, wqrt_ref, kvg_ref, wk_ref, wvt_ref,
                cosk_ref, sink_ref, cosq_ref, sinq_ref,
                hc_ref, q_ref, k_ref, vt_ref):
    x = x_ref[...]
    h = _rms(x, g_ref[...]).astype(BF16)
    z = _dot(h, win_ref[...])
    hc_ref[...] = z[:, :D_CONV] * jax.nn.sigmoid(z[:, D_CONV:2 * D_CONV])
    cqn = _rms(z[:, OFF_CQ:OFF_CKV], qg_ref[...]).astype(BF16)
    ckvn = _rms(z[:, OFF_CKV:OFF_KR], kvg_ref[...]).astype(BF16)
    kr = (z[:, OFF_KR:OFF_KR + LANES] * cosk_ref[...]
          + z[:, OFF_KR + LANES:OFF_KR + 2 * LANES] * sink_ref[...])
    kall = _dot(ckvn, wk_ref[...])
    for hd in range(N_HEADS):
        sl = slice(hd * HEAD_PAD, (hd + 1) * HEAD_PAD)
        k_ref[:, sl] = (kall[:, sl] + kr).astype(BF16)
    vt = _dot_nt(wvt_ref[...], ckvn)
    ones = jnp.ones((V_ROWS - V_HEAD_DIM, x.shape[0]), BF16)
    for hd in range(N_HEADS):
        vt_ref[hd, 0:V_HEAD_DIM, :] = vt[hd * V_HEAD_DIM:(hd + 1) * V_HEAD_DIM].astype(BF16)
        vt_ref[hd, V_HEAD_DIM:V_ROWS, :] = ones
    qt = _dot_nt(wqt_ref[...], cqn)
    qrt = _dot_nt(wqrt_ref[...], cqn)
    cq = cosq_ref[...]
    sq = sinq_ref[...]
    zeros = jnp.zeros((HEAD_PAD - QK_HEAD_DIM, x.shape[0]), BF16)
    for hd in range(N_HEADS):
        base = hd * HEAD_PAD
        q_ref[hd, 0:QK_NOPE_DIM, :] = (qt[base:base + QK_NOPE_DIM] * Q_SCALE).astype(BF16)
        rope = (qt[base + QK_NOPE_DIM:base + QK_HEAD_DIM] * cq
                + qrt[hd * QK_ROPE_DIM:(hd + 1) * QK_ROPE_DIM] * sq)
        q_ref[hd, QK_NOPE_DIM:QK_HEAD_DIM, :] = rope.astype(BF16)
        q_ref[hd, QK_HEAD_DIM:HEAD_PAD, :] = zeros


def _pre_call(x, lw, tabs, segs, tm):
    T = x.shape[0]
    n_tiles = T // tm
    pos_map = _pos_block_map(segs, tm)
    full = lambda shape: pl.BlockSpec(shape, lambda i: (0,) * len(shape))
    in_cols = lw["w_in"].shape[1]
    return pl.pallas_call(
        _pre_kernel,
        grid=(n_tiles,),
        in_specs=[
            pl.BlockSpec((tm, D_MODEL), lambda i: (i, 0)),
            full((1, D_MODEL)),
            full((D_MODEL, in_cols)),
            full((1, Q_LORA_RANK)),
            full((N_HEADS * HEAD_PAD, Q_LORA_RANK)),
            full((N_HEADS * QK_ROPE_DIM, Q_LORA_RANK)),
            full((1, KV_LORA_RANK)),
            full((KV_LORA_RANK, N_HEADS * HEAD_PAD)),
            full((N_HEADS * V_HEAD_DIM, KV_LORA_RANK)),
            pl.BlockSpec((tm, LANES), lambda i: (pos_map(i), 0)),
            pl.BlockSpec((tm, LANES), lambda i: (pos_map(i), 0)),
            pl.BlockSpec((QK_ROPE_DIM, tm), lambda i: (0, pos_map(i))),
            pl.BlockSpec((QK_ROPE_DIM, tm), lambda i: (0, pos_map(i))),
        ],
        out_specs=[
            pl.BlockSpec((tm, D_CONV), lambda i: (i, 0)),
            pl.BlockSpec((N_HEADS, HEAD_PAD, tm), lambda i: (0, 0, i)),
            pl.BlockSpec((tm, N_HEADS * HEAD_PAD), lambda i: (i, 0)),
            pl.BlockSpec((N_HEADS, V_ROWS, tm), lambda i: (0, 0, i)),
        ],
        out_shape=[
            jax.ShapeDtypeStruct((T, D_CONV), F32),
            jax.ShapeDtypeStruct((N_HEADS, HEAD_PAD, T), BF16),
            jax.ShapeDtypeStruct((T, N_HEADS * HEAD_PAD), BF16),
            jax.ShapeDtypeStruct((N_HEADS, V_ROWS, T), BF16),
        ],
        compiler_params=_cparams("parallel"),
        name="layer_front",
    )(x, lw["attn_norm_g"], lw["w_in"], lw["q_norm_g"], lw["wq_t"], lw["wqrot_t"],
      lw["kv_norm_g"], lw["wk"], lw["wv_t"], tabs["cosk"], tabs["sink"], tabs["cosq_t"],
      tabs["sinq_t"])


def _pos_block_map(segs, tile):
    def pos_map(i):
        out = None
        for off, n_seq, seq_len in reversed(segs):
            blk = (i - off // tile) % (seq_len // tile)
            out = blk if out is None else jnp.where(i < (off + n_seq * seq_len) // tile, blk, out)
        return out
    return pos_map


def _attn_kernel(q_ref, k_ref, vt_ref, o_ref, s_ref, cm_ref, *, tq, tk, nb):
    seq = k_ref.shape[0]
    n_kv = seq // tk
    n_body = n_kv // nb
    n_q = seq // tq
    look = nb - 1

    def q_tile(qi):
        return q_ref[0, :, pl.ds(pl.multiple_of(qi * tq, tq), tq)]

    def stage_a(q, j, slot):
        start = pl.multiple_of(j * tk, tk)
        s = _dot(k_ref[pl.ds(start, tk), :], q)
        s_ref[slot] = s
        cm_ref[slot] = jnp.max(s, axis=0, keepdims=True)

    def stage_bc(j, slot, m, acc):
        m_new = jnp.maximum(m, cm_ref[slot])
        alpha = jnp.exp2(m - m_new)
        p = jnp.exp2(s_ref[slot] - m_new).astype(BF16)
        start = pl.multiple_of(j * tk, tk)
        acc = alpha * acc + _dot(vt_ref[0, :, pl.ds(start, tk)], p)
        return m_new, acc

    def body(q_cur, q_next, j0, carry, last=False):
        m, acc = carry
        for i in range(nb):
            if last and i >= 1:
                stage_a(q_next, i - 1, (i + look) % nb)
            else:
                stage_a(q_cur, j0 + i + look, (i + look) % nb)
            m, acc = stage_bc(j0 + i, i, m, acc)
        return m, acc

    q_first = q_tile(0)
    for j in range(look):
        stage_a(q_first, j, j)

    def per_query_tile(qi, _):
        q_cur = q_tile(qi)
        q_next = q_tile(jnp.minimum(qi + 1, n_q - 1))
        carry = (jnp.full((1, tq), NEG_BIG, F32), jnp.zeros((V_ROWS, tq), F32))
        if n_body > 1:
            carry = lax.fori_loop(0, n_body - 1,
                                  lambda jb, c: body(q_cur, q_next, jb * nb, c), carry)
        _, acc = body(q_cur, q_next, (n_body - 1) * nb, carry, last=True)
        out = acc[:V_HEAD_DIM] / acc[V_HEAD_DIM:V_HEAD_DIM + 1]
        o_ref[:, pl.ds(pl.multiple_of(qi * tq, tq), tq)] = out.astype(o_ref.dtype)
        return 0

    lax.fori_loop(0, n_q, per_query_tile, 0)


def _attn_call(q_t, k_all, v_t, tok_off, n_seq, seq_len, tq, tk, nb):
    assert (seq_len // tk) % nb == 0 and seq_len % tq == 0
    seq0 = tok_off // seq_len
    return pl.pallas_call(
        functools.partial(_attn_kernel, tq=tq, tk=tk, nb=nb),
        grid=(n_seq, N_HEADS),
        in_specs=[
            pl.BlockSpec((1, HEAD_PAD, seq_len), lambda b, h: (h, 0, seq0 + b)),
            pl.BlockSpec((seq_len, HEAD_PAD), lambda b, h: (seq0 + b, h)),
            pl.BlockSpec((1, V_ROWS, seq_len), lambda b, h: (h, 0, seq0 + b)),
        ],
        out_specs=pl.BlockSpec((V_HEAD_DIM, seq_len), lambda b, h: (h, b)),
        out_shape=jax.ShapeDtypeStruct((N_HEADS * V_HEAD_DIM, n_seq * seq_len), BF16),
        scratch_shapes=[pltpu.VMEM((nb, tk, tq), F32), pltpu.VMEM((nb, 1, tq), F32)],
        compiler_params=_cparams("parallel", "parallel"),
        name="mla_attention",
    )(q_t, k_all, v_t)


def _conv_kernel(cur_ref, prev_ref, next_ref, w_ref, b_ref, lng_ref, lnb_ref, og_ref, o_ref,
                 ext_ref, *, seg_tiles, rows):
    i = pl.program_id(0)
    tc = cur_ref.shape[0]
    is_start = _tile_is_boundary(i, seg_tiles, 0)
    is_end = _tile_is_boundary(i, seg_tiles, 1)
    ext_ref[0:HALO, :] = jnp.where(is_start, 0.0, prev_ref[...])
    ext_ref[HALO:HALO + tc, :] = cur_ref[...]
    ext_ref[HALO + tc:2 * HALO + tc, :] = jnp.where(is_end, 0.0, next_ref[...])
    bias = b_ref[...]
    lng = lng_ref[...]
    lnb = lnb_ref[...]
    og = og_ref[...]
    for c in range(tc // rows):
        base = HALO - CONV_PAD + c * rows
        acc = jnp.broadcast_to(bias, (rows, D_CONV))
        for k in range(CONV_WIDTH):
            acc = acc + w_ref[k:k + 1, :] * ext_ref[base + k:base + k + rows, :]
        mu = jnp.mean(acc, axis=-1, keepdims=True)
        d = acc - mu
        var = jnp.mean(d * d, axis=-1, keepdims=True)
        y = d * lax.rsqrt(var + EPS) * lng + lnb
        y = y * jax.nn.sigmoid(y)
        o_ref[c * rows:(c + 1) * rows, :] = _rms(y, og).astype(o_ref.dtype)


def _tile_is_boundary(i, seg_tiles, end):
    out = None
    for first, n, per in reversed(seg_tiles):
        rel = (i - first + end) % per == 0
        out = rel if out is None else jnp.where(i < first + n, rel, out)
    return out


def _conv_call(hc, lw, segs, tc, rows):
    T = hc.shape[0]
    n_tiles = T // tc
    hb = tc // HALO
    last_halo = T // HALO - 1
    seg_tiles = tuple((off // tc, n * s // tc, s // tc) for off, n, s in segs)
    full = lambda shape: pl.BlockSpec(shape, lambda i: (0,) * len(shape))
    return pl.pallas_call(
        functools.partial(_conv_kernel, seg_tiles=seg_tiles, rows=rows),
        grid=(n_tiles,),
        in_specs=[
            pl.BlockSpec((tc, D_CONV), lambda i: (i, 0)),
            pl.BlockSpec((HALO, D_CONV), lambda i: (jnp.maximum(i * hb - 1, 0), 0)),
            pl.BlockSpec((HALO, D_CONV), lambda i: (jnp.minimum((i + 1) * hb, last_halo), 0)),
            full((CONV_WIDTH, D_CONV)),
            full((1, D_CONV)), full((1, D_CONV)), full((1, D_CONV)), full((1, D_CONV)),
        ],
        out_specs=pl.BlockSpec((tc, D_CONV), lambda i: (i, 0)),
        out_shape=jax.ShapeDtypeStruct((T, D_CONV), BF16),
        scratch_shapes=[pltpu.VMEM((tc + 2 * HALO, D_CONV), F32)],
        compiler_params=_cparams("parallel"),
        name="conv_branch",
    )(hc, hc, hc, lw["conv_dw_w"], lw["conv_dw_b"], lw["conv_ln_g"], lw["conv_ln_b"],
      lw["conv_out_norm_g"])


def _post_kernel(ot_ref, oc_ref, x_ref, ga_ref, wa_ref, wc_ref, y_ref):
    ot = ot_ref[...].astype(F32)
    r = lax.rsqrt(jnp.mean(ot * ot, axis=0, keepdims=True) + EPS)
    oa = (jnp.transpose(ot * r) * ga_ref[...]).astype(BF16)
    y_ref[...] = x_ref[...] + _dot(oa, wa_ref[...]) + _dot(oc_ref[...], wc_ref[...])


def _post_call(o_t, ocn, x, lw, tm):
    T = x.shape[0]
    full = lambda shape: pl.BlockSpec(shape, lambda i: (0,) * len(shape))
    return pl.pallas_call(
        _post_kernel,
        grid=(T // tm,),
        in_specs=[
            pl.BlockSpec((D_ATTN, tm), lambda i: (0, i)),
            pl.BlockSpec((tm, D_CONV), lambda i: (i, 0)),
            pl.BlockSpec((tm, D_MODEL), lambda i: (i, 0)),
            full((1, D_ATTN)),
            full((D_ATTN, D_MODEL)),
            full((D_CONV, D_MODEL)),
        ],
        out_specs=pl.BlockSpec((tm, D_MODEL), lambda i: (i, 0)),
        out_shape=jax.ShapeDtypeStruct((T, D_MODEL), F32),
        compiler_params=_cparams("parallel"),
        name="mixer_out",
    )(o_t, ocn, x, lw["attn_out_norm_g"], lw["w_out_a"], lw["w_out_c"])


def _dense_ffn_kernel(x_ref, g_ref, wg_ref, wu_ref, wd_ref, y_ref, h_ref, acc_ref):
    j = pl.program_id(1)

    @pl.when(j == 0)
    def _():
        x = x_ref[...]
        h_ref[...] = _rms(x, g_ref[...]).astype(BF16)
        acc_ref[...] = x

    h = h_ref[...]
    gate = _dot(h, wg_ref[...])
    up = _dot(h, wu_ref[...])
    a = (gate * jax.nn.sigmoid(gate) * up).astype(BF16)
    acc_ref[...] += _dot(a, wd_ref[...])

    @pl.when(j == pl.num_programs(1) - 1)
    def _():
        y_ref[...] = acc_ref[...]


def _dense_ffn_call(x, g, wg, wu, wd, tm, tf):
    T = x.shape[0]
    return pl.pallas_call(
        _dense_ffn_kernel,
        grid=(T // tm, D_FF // tf),
        in_specs=[
            pl.BlockSpec((tm, D_MODEL), lambda i, j: (i, 0)),
            pl.BlockSpec((1, D_MODEL), lambda i, j: (0, 0)),
            pl.BlockSpec((D_MODEL, tf), lambda i, j: (0, j)),
            pl.BlockSpec((D_MODEL, tf), lambda i, j: (0, j)),
            pl.BlockSpec((tf, D_MODEL), lambda i, j: (j, 0)),
        ],
        out_specs=pl.BlockSpec((tm, D_MODEL), lambda i, j: (i, 0)),
        out_shape=jax.ShapeDtypeStruct((T, D_MODEL), F32),
        scratch_shapes=[pltpu.VMEM((tm, D_MODEL), BF16), pltpu.VMEM((tm, D_MODEL), F32)],
        compiler_params=_cparams("parallel", "arbitrary"),
        name="dense_ffn",
    )(x, g, wg, wu, wd)


MOE_CHUNK = 128


def _router_top2_t(h_f32, wrt_hi, wrt_lo, br_col):
    h_hi = h_f32.astype(BF16)
    h_lo = (h_f32 - h_hi.astype(F32)).astype(BF16)
    logits = (_dot_nt(wrt_hi, h_hi) + _dot_nt(wrt_hi, h_lo) + _dot_nt(wrt_lo, h_hi))[:N_EXPERTS]
    logits = logits + br_col
    mx = jnp.max(logits, axis=0, keepdims=True)
    ex = jnp.exp(logits - mx)
    probs = ex / jnp.sum(ex, axis=0, keepdims=True)
    row = lax.broadcasted_iota(jnp.int32, probs.shape, 0).astype(F32)
    none = float(N_EXPERTS)
    p1 = jnp.max(probs, axis=0, keepdims=True)
    i1 = jnp.min(jnp.where(probs == p1, row, none), axis=0, keepdims=True)
    sel1 = row == i1
    rest = jnp.where(sel1, -1.0, probs)
    p2 = jnp.max(rest, axis=0, keepdims=True)
    i2 = jnp.min(jnp.where(rest == p2, row, none), axis=0, keepdims=True)
    sel2 = row == i2
    denom = p1 + p2
    gates = jnp.where(sel1, p1 / denom, 0.0) + jnp.where(sel2, p2 / denom, 0.0)
    mask = jnp.where(sel1 | sel2, 1.0, 0.0)
    return gates, mask


def _moe_kernel(x_ref, g_ref, wrh_ref, wrl_ref, br_ref, tri_ref, wg_ref, wu_ref, wd_ref, y_ref,
                h_ref, gates_ref, mask_ref, rank_ref, acc_ref, cnt_ref):
    e = pl.program_id(1)
    tm = x_ref.shape[0]

    @pl.when(e == 0)
    def _():
        x = x_ref[...]
        hf = _rms(x, g_ref[...])
        h_ref[...] = hf.astype(BF16)
        gates, mask = _router_top2_t(hf, wrh_ref[...], wrl_ref[...], br_ref[...])
        gates_ref[...] = gates
        mask_ref[...] = mask
        mask16 = jnp.concatenate([mask, jnp.zeros_like(mask)], axis=0).astype(BF16)
        rank_ref[...] = _dot(mask16, tri_ref[...])[:N_EXPERTS]
        for ex in range(N_EXPERTS):
            cnt_ref[ex] = jnp.sum(mask[ex:ex + 1, :]).astype(jnp.int32)
        acc_ref[...] = x

    n_rows = cnt_ref[e]
    n_chunks = lax.shift_right_logical(n_rows + (MOE_CHUNK - 1), int(math.log2(MOE_CHUNK)))
    rank_e = rank_ref[pl.ds(e, 1), :]
    routed_e = mask_ref[pl.ds(e, 1), :] > 0.5
    gate_e = gates_ref[pl.ds(e, 1), :]
    slot = lax.broadcasted_iota(jnp.int32, (MOE_CHUNK, tm), 0).astype(F32)

    def chunk(c, _):
        first = (c * MOE_CHUNK).astype(F32)
        sel = (rank_e == slot + first) & routed_e
        sel_b = jnp.where(sel, 1.0, 0.0).astype(BF16)
        xs = _dot(sel_b, h_ref[...]).astype(BF16)
        gate = _dot(xs, wg_ref[0])
        up = _dot(xs, wu_ref[0])
        a = (gate * jax.nn.sigmoid(gate) * up).astype(BF16)
        out = _dot(a, wd_ref[0])
        row_gate = jnp.sum(jnp.where(sel, gate_e, 0.0), axis=1, keepdims=True)
        scaled = (out * row_gate).astype(BF16)
        acc_ref[...] += lax.dot_general(sel_b, scaled, (((0,), (0,)), ((), ())),
                                        preferred_element_type=F32)
        return 0

    lax.fori_loop(0, n_chunks, chunk, 0)

    @pl.when(e == pl.num_programs(1) - 1)
    def _():
        y_ref[...] = acc_ref[...]


def _moe_call(x, g, wrt_hi, wrt_lo, br_col, wg, wu, wd, tm):
    T = x.shape[0]
    full2 = lambda shape: pl.BlockSpec(shape, lambda i, e: (0,) * len(shape))
    tri = (lax.broadcasted_iota(jnp.int32, (tm, tm), 0)
           < lax.broadcasted_iota(jnp.int32, (tm, tm), 1)).astype(BF16)
    return pl.pallas_call(
        _moe_kernel,
        grid=(T // tm, N_EXPERTS),
        in_specs=[
            pl.BlockSpec((tm, D_MODEL), lambda i, e: (i, 0)),
            full2((1, D_MODEL)),
            full2((LANES, D_MODEL)),
            full2((LANES, D_MODEL)),
            full2((N_EXPERTS, 1)),
            full2((tm, tm)),
            pl.BlockSpec((1, D_MODEL, D_EXPERT), lambda i, e: (e, 0, 0)),
            pl.BlockSpec((1, D_MODEL, D_EXPERT), lambda i, e: (e, 0, 0)),
            pl.BlockSpec((1, D_EXPERT, D_MODEL), lambda i, e: (e, 0, 0)),
        ],
        out_specs=pl.BlockSpec((tm, D_MODEL), lambda i, e: (i, 0)),
        out_shape=jax.ShapeDtypeStruct((T, D_MODEL), F32),
        scratch_shapes=[pltpu.VMEM((tm, D_MODEL), BF16),
                        pltpu.VMEM((N_EXPERTS, tm), F32), pltpu.VMEM((N_EXPERTS, tm), F32),
                        pltpu.VMEM((N_EXPERTS, tm), F32),
                        pltpu.VMEM((tm, D_MODEL), F32),
                        pltpu.SMEM((N_EXPERTS,), jnp.int32)],
        compiler_params=_cparams("parallel", "arbitrary"),
        name="moe_ffn",
    )(x, g, wrt_hi, wrt_lo, br_col, tri, wg, wu, wd)


def _final_norm_kernel(x_ref, g_ref, y_ref):
    y_ref[...] = _rms(x_ref[...], g_ref[...])


def _final_norm_call(x, g, tok_off, n_tok, tm):
    tile0 = tok_off // tm
    return pl.pallas_call(
        _final_norm_kernel,
        grid=(n_tok // tm,),
        in_specs=[pl.BlockSpec((tm, D_MODEL), lambda i: (tile0 + i, 0)),
                  pl.BlockSpec((1, D_MODEL), lambda i: (0, 0))],
        out_specs=pl.BlockSpec((tm, D_MODEL), lambda i: (i, 0)),
        out_shape=jax.ShapeDtypeStruct((n_tok, D_MODEL), F32),
        compiler_params=_cparams("parallel"),
        name="final_norm",
    )(x, g)


def _rope_tables(max_len):
    pos = jnp.arange(max_len, dtype=F32)
    freqs = ROPE_THETA ** (-(jnp.arange(0, QK_ROPE_DIM, 2, dtype=F32) / QK_ROPE_DIM))
    ang = pos[:, None] * freqs[None, :]
    ang = jnp.concatenate([ang, ang], axis=-1)
    cos, sin = jnp.cos(ang), jnp.sin(ang)
    pad = lambda t: jnp.pad(t, ((0, 0), (QK_NOPE_DIM, HEAD_PAD - QK_HEAD_DIM)))
    return {"cosk": pad(cos), "sink": pad(sin),
            "cosq_t": (cos * Q_SCALE).T, "sinq_t": (sin * Q_SCALE).T}


def _rot_cols(w):
    half = QK_ROPE_DIM // 2
    return jnp.concatenate([-w[..., half:], w[..., :half]], axis=-1)


def _layer_weights(i, p):
    row = lambda v: v.reshape(1, -1)
    w_in = p["w_in"][i]
    w_kr = w_in[:, OFF_KR:]
    place = lambda w: jnp.pad(w, ((0, 0), (QK_NOPE_DIM, HEAD_PAD - QK_HEAD_DIM)))
    w_in_ext = jnp.concatenate([w_in[:, :OFF_KR], place(w_kr), place(_rot_cols(w_kr))], axis=1)
    w_uq = p["w_uq"][i].reshape(Q_LORA_RANK, N_HEADS, QK_HEAD_DIM)
    wq_pad = jnp.pad(w_uq, ((0, 0), (0, 0), (0, HEAD_PAD - QK_HEAD_DIM)))
    wq_t = wq_pad.reshape(Q_LORA_RANK, N_HEADS * HEAD_PAD).T
    wqrot_t = _rot_cols(w_uq[..., QK_NOPE_DIM:]).reshape(Q_LORA_RANK, N_HEADS * QK_ROPE_DIM).T
    w_ukv = p["w_ukv"][i].reshape(KV_LORA_RANK, N_HEADS, QK_NOPE_DIM + V_HEAD_DIM)
    wk = jnp.pad(w_ukv[..., :QK_NOPE_DIM], ((0, 0), (0, 0), (0, HEAD_PAD - QK_NOPE_DIM)))
    wk = wk.reshape(KV_LORA_RANK, N_HEADS * HEAD_PAD)
    wv_t = w_ukv[..., QK_NOPE_DIM:].reshape(KV_LORA_RANK, N_HEADS * V_HEAD_DIM).T
    w_out = p["w_out"][i]
    return {
        "attn_norm_g": row(p["attn_norm_g"][i]),
        "w_in": w_in_ext.astype(BF16),
        "q_norm_g": row(p["q_norm_g"][i]),
        "wq_t": wq_t.astype(BF16),
        "wqrot_t": wqrot_t.astype(BF16),
        "kv_norm_g": row(p["kv_norm_g"][i]),
        "wk": wk.astype(BF16),
        "wv_t": wv_t.astype(BF16),
        "conv_dw_w": p["conv_dw_w"][i].reshape(CONV_WIDTH, D_CONV),
        "conv_dw_b": row(p["conv_dw_b"][i]),
        "conv_ln_g": row(p["conv_ln_g"][i]),
        "conv_ln_b": row(p["conv_ln_b"][i]),
        "conv_out_norm_g": row(p["conv_out_norm_g"][i]),
        "attn_out_norm_g": row(p["attn_out_norm_g"][i]),
        "w_out_a": w_out[:D_ATTN].astype(BF16),
        "w_out_c": w_out[D_ATTN:].astype(BF16),
        "ffn_norm_g": row(p["ffn_norm_g"][i]),
    }


def _split_hi_lo(w):
    hi = w.astype(BF16)
    lo = (w - hi.astype(F32)).astype(BF16)
    return hi, lo


def _tile_sizes(segs):
    s_min = min(s for _, _, s in segs)
    return {
        "tm": min(512, s_min),
        "tc": min(512, s_min),
        "rows": 64,
        "tq": min(512, s_min),
        "tk": min(512, s_min),
        "nb": 4,
        "tf": D_FF // 2,
        "tme": min(1024, s_min),
    }


def kernel(x_prompt, x_sample, attn_norm_g, w_in, q_norm_g, w_uq, kv_norm_g, w_ukv, conv_dw_w,
           conv_dw_b, conv_ln_g, conv_ln_b, attn_out_norm_g, conv_out_norm_g, w_out, ffn_norm_g,
           dense_w_gate, dense_w_up, dense_w_down, moe_w_router, moe_b_router, moe_w_gate,
           moe_w_up, moe_w_down, final_norm_g):
    p = dict(attn_norm_g=attn_norm_g, w_in=w_in, q_norm_g=q_norm_g, w_uq=w_uq,
             kv_norm_g=kv_norm_g, w_ukv=w_ukv, conv_dw_w=conv_dw_w, conv_dw_b=conv_dw_b,
             conv_ln_g=conv_ln_g, conv_ln_b=conv_ln_b, attn_out_norm_g=attn_out_norm_g,
             conv_out_norm_g=conv_out_norm_g, w_out=w_out, ffn_norm_g=ffn_norm_g)
    bp, sp, _ = x_prompt.shape
    bs, ss, _ = x_sample.shape
    n_p, n_s = bp * sp, bs * ss
    segs = ((0, bp, sp), (n_p, bs, ss))
    assert n_p % ss == 0 and n_p % sp == 0
    ts = _tile_sizes(segs)
    tabs = _rope_tables(max(sp, ss))
    x = jnp.concatenate([x_prompt.reshape(n_p, D_MODEL), x_sample.reshape(n_s, D_MODEL)], axis=0)

    for i in range(DEPTH):
        lw = _layer_weights(i, p)
        hc, q_t, k_all, v_t = _pre_call(x, lw, tabs, segs, ts["tm"])
        o_t = jnp.concatenate(
            [_attn_call(q_t, k_all, v_t, off, n, s, ts["tq"], ts["tk"],
                        math.gcd(ts["nb"], s // ts["tk"])) for off, n, s in segs],
            axis=1)
        ocn = _conv_call(hc, lw, segs, ts["tc"], ts["rows"])
        x = _post_call(o_t, ocn, x, lw, ts["tm"])
        j = i // 2
        if i % 2 == 0:
            x = _dense_ffn_call(x, lw["ffn_norm_g"], dense_w_gate[j].astype(BF16),
                                dense_w_up[j].astype(BF16), dense_w_down[j].astype(BF16),
                                ts["tm"], ts["tf"])
        else:
            wrt = jnp.pad(moe_w_router[j].T, ((0, LANES - N_EXPERTS), (0, 0)))
            wrt_hi, wrt_lo = _split_hi_lo(wrt)
            x = _moe_call(x, lw["ffn_norm_g"], wrt_hi, wrt_lo,
                          moe_b_router[j].reshape(N_EXPERTS, 1),
                          moe_w_gate[j].astype(BF16), moe_w_up[j].astype(BF16),
                          moe_w_down[j].astype(BF16), ts["tme"])

    g = final_norm_g.reshape(1, D_MODEL)
    y_p = _final_norm_call(x, g, 0, n_p, ts["tm"]).reshape(bp, sp, D_MODEL)
    y_s = _final_norm_call(x, g, n_p, n_s, ts["tm"]).reshape(bs, ss, D_MODEL)
    return (y_p, y_s)
```

```python
import functools
import math

import jax
import jax.numpy as jnp
import numpy as np
from jax import lax
from jax.experimental import pallas as pl
from jax.experimental.pallas import tpu as pltpu

D_MODEL = 1024
DEPTH = 4
D_ATTN = 512
D_CONV = 512
N_HEADS = 8
QK_NOPE_DIM = 64
QK_ROPE_DIM = 32
V_HEAD_DIM = 64
QK_HEAD_DIM = QK_NOPE_DIM + QK_ROPE_DIM
Q_LORA_RANK = 256
KV_LORA_RANK = 128
ROPE_THETA = 10000.0
CONV_WIDTH = 31
CONV_PAD = (CONV_WIDTH - 1) // 2
D_FF = 2816
N_EXPERTS = 8
D_EXPERT = 1024
EPS = 1e-6
OFF_CQ = 2 * D_CONV
OFF_CKV = OFF_CQ + Q_LORA_RANK
OFF_KR = OFF_CKV + KV_LORA_RANK

LANES = 128
SUBLANES = 8
HEAD_PAD = LANES
HALO = 16
BF16_SUBLANES = 16
V_ROWS = V_HEAD_DIM + BF16_SUBLANES
VMEM_LIMIT_BYTES = 56 * 1024 * 1024

BF16 = jnp.bfloat16
F32 = jnp.float32
NEG_BIG = -1e30
Q_SCALE = (QK_HEAD_DIM ** -0.5) * math.log2(math.e)


def _cparams(*sem):
    return pltpu.CompilerParams(dimension_semantics=sem, vmem_limit_bytes=VMEM_LIMIT_BYTES)


def _rms(x, g):
    return x * lax.rsqrt(jnp.mean(x * x, axis=-1, keepdims=True) + EPS) * g


def _dot(a, b):
    return jnp.dot(a, b, preferred_element_type=F32)


def _dot_nt(a, b):
    return lax.dot_general(a, b, (((1,), (1,)), ((), ())), preferred_element_type=F32)


def _pre_kernel(x_ref, g_ref, win_ref, qg_ref, wqt_ref, wqrt_ref, kvg_ref, wk_ref, wvt_ref,
                cosk_ref, sink_ref, cosq_ref, sinq_ref,
                hc_ref, q_ref, k_ref, vt_ref):
    x = x_ref[...]
    h = _rms(x, g_ref[...]).astype(BF16)
    z = _dot(h, win_ref[...])
    hc_ref[...] = z[:, :D_CONV] * jax.nn.sigmoid(z[:, D_CONV:2 * D_CONV])
    cqn = _rms(z[:, OFF_CQ:OFF_CKV], qg_ref[...]).astype(BF16)
    ckvn = _rms(z[:, OFF_CKV:OFF_KR], kvg_ref[...]).astype(BF16)
    kr = (z[:, OFF_KR:OFF_KR + LANES] * cosk_ref[...]
          + z[:, OFF_KR + LANES:OFF_KR + 2 * LANES] * sink_ref[...])
    kall = _dot(ckvn, wk_ref[...])
    for hd in range(N_HEADS):
        sl = slice(hd * HEAD_PAD, (hd + 1) * HEAD_PAD)
        k_ref[:, sl] = (kall[:, sl] + kr).astype(BF16)
    vt = _dot_nt(wvt_ref[...], ckvn)
    ones = jnp.ones((V_ROWS - V_HEAD_DIM, x.shape[0]), BF16)
    for hd in range(N_HEADS):
        vt_ref[hd, 0:V_HEAD_DIM, :] = vt[hd * V_HEAD_DIM:(hd + 1) * V_HEAD_DIM].astype(BF16)
        vt_ref[hd, V_HEAD_DIM:V_ROWS, :] = ones
    qt = _dot_nt(wqt_ref[...], cqn)
    qrt = _dot_nt(wqrt_ref[...], cqn)
    cq = cosq_ref[...]
    sq = sinq_ref[...]
    zeros = jnp.zeros((HEAD_PAD - QK_HEAD_DIM, x.shape[0]), BF16)
    for hd in range(N_HEADS):
        base = hd * HEAD_PAD
        q_ref[hd, 0:QK_NOPE_DIM, :] = (qt[base:base + QK_NOPE_DIM] * Q_SCALE).astype(BF16)
        rope = (qt[base + QK_NOPE_DIM:base + QK_HEAD_DIM] * cq
                + qrt[hd * QK_ROPE_DIM:(hd + 1) * QK_ROPE_DIM] * sq)
        q_ref[hd, QK_NOPE_DIM:QK_HEAD_DIM, :] = rope.astype(BF16)
        q_ref[hd, QK_HEAD_DIM:HEAD_PAD, :] = zeros


def _pre_call(x, lw, tabs, segs, tm):
    T = x.shape[0]
    n_tiles = T // tm
    pos_map = _pos_block_map(segs, tm)
    full = lambda shape: pl.BlockSpec(shape, lambda i: (0,) * len(shape))
    in_cols = lw["w_in"].shape[1]
    return pl.pallas_call(
        _pre_kernel,
        grid=(n_tiles,),
        in_specs=[
            pl.BlockSpec((tm, D_MODEL), lambda i: (i, 0)),
            full((1, D_MODEL)),
            full((D_MODEL, in_cols)),
            full((1, Q_LORA_RANK)),
            full((N_HEADS * HEAD_PAD, Q_LORA_RANK)),
            full((N_HEADS * QK_ROPE_DIM, Q_LORA_RANK)),
            full((1, KV_LORA_RANK)),
            full((KV_LORA_RANK, N_HEADS * HEAD_PAD)),
            full((N_HEADS * V_HEAD_DIM, KV_LORA_RANK)),
            pl.BlockSpec((tm, LANES), lambda i: (pos_map(i), 0)),
            pl.BlockSpec((tm, LANES), lambda i: (pos_map(i), 0)),
            pl.BlockSpec((QK_ROPE_DIM, tm), lambda i: (0, pos_map(i))),
            pl.BlockSpec((QK_ROPE_DIM, tm), lambda i: (0, pos_map(i))),
        ],
        out_specs=[
            pl.BlockSpec((tm, D_CONV), lambda i: (i, 0)),
            pl.BlockSpec((N_HEADS, HEAD_PAD, tm), lambda i: (0, 0, i)),
            pl.BlockSpec((tm, N_HEADS * HEAD_PAD), lambda i: (i, 0)),
            pl.BlockSpec((N_HEADS, V_ROWS, tm), lambda i: (0, 0, i)),
        ],
        out_shape=[
            jax.ShapeDtypeStruct((T, D_CONV), F32),
            jax.ShapeDtypeStruct((N_HEADS, HEAD_PAD, T), BF16),
            jax.ShapeDtypeStruct((T, N_HEADS * HEAD_PAD), BF16),
            jax.ShapeDtypeStruct((N_HEADS, V_ROWS, T), BF16),
        ],
        compiler_params=_cparams("parallel"),
        name="layer_front",
    )(x, lw["attn_norm_g"], lw["w_in"], lw["q_norm_g"], lw["wq_t"], lw["wqrot_t"],
      lw["kv_norm_g"], lw["wk"], lw["wv_t"], tabs["cosk"], tabs["sink"], tabs["cosq_t"],
      tabs["sinq_t"])


def _pos_block_map(segs, tile):
    def pos_map(i):
        out = None
        for off, n_seq, seq_len in reversed(segs):
            blk = (i - off // tile) % (seq_len // tile)
            out = blk if out is None else jnp.where(i < (off + n_seq * seq_len) // tile, blk, out)
        return out
    return pos_map


def _attn_kernel(q_ref, k_ref, vt_ref, o_ref, s_ref, cm_ref, *, tq, tk, nb):
    seq = k_ref.shape[0]
    n_kv = seq // tk
    n_body = n_kv // nb
    n_q = seq // tq
    look = nb - 1

    def q_tile(qi):
        return q_ref[0, :, pl.ds(pl.multiple_of(qi * tq, tq), tq)]

    def stage_a(q, j, slot):
        start = pl.multiple_of(j * tk, tk)
        s = _dot(k_ref[pl.ds(start, tk), :], q)
        s_ref[slot] = s
        cm_ref[slot] = jnp.max(s, axis=0, keepdims=True)

    def stage_bc(j, slot, m, acc):
        m_new = jnp.maximum(m, cm_ref[slot])
        alpha = jnp.exp2(m - m_new)
        p = jnp.exp2(s_ref[slot] - m_new).astype(BF16)
        start = pl.multiple_of(j * tk, tk)
        acc = alpha * acc + _dot(vt_ref[0, :, pl.ds(start, tk)], p)
        return m_new, acc

    def body(q_cur, q_next, j0, carry, last=False):
        m, acc = carry
        for i in range(nb):
            if last and i >= 1:
                stage_a(q_next, i - 1, (i + look) % nb)
            else:
                stage_a(q_cur, j0 + i + look, (i + look) % nb)
            m, acc = stage_bc(j0 + i, i, m, acc)
        return m, acc

    q_first = q_tile(0)
    for j in range(look):
        stage_a(q_first, j, j)

    def per_query_tile(qi, _):
        q_cur = q_tile(qi)
        q_next = q_tile(jnp.minimum(qi + 1, n_q - 1))
        carry = (jnp.full((1, tq), NEG_BIG, F32), jnp.zeros((V_ROWS, tq), F32))
        if n_body > 1:
            carry = lax.fori_loop(0, n_body - 1,
                                  lambda jb, c: body(q_cur, q_next, jb * nb, c), carry)
        _, acc = body(q_cur, q_next, (n_body - 1) * nb, carry, last=True)
        out = acc[:V_HEAD_DIM] / acc[V_HEAD_DIM:V_HEAD_DIM + 1]
        o_ref[:, pl.ds(pl.multiple_of(qi * tq, tq), tq)] = out.astype(o_ref.dtype)
        return 0

    lax.fori_loop(0, n_q, per_query_tile, 0)


def _attn_call(q_t, k_all, v_t, o_prev, tok_off, n_seq, seq_len, tq, tk, nb):
    assert (seq_len // tk) % nb == 0 and seq_len % tq == 0
    T = k_all.shape[0]
    seq0 = tok_off // seq_len
    in_specs = [
        pl.BlockSpec((1, HEAD_PAD, seq_len), lambda b, h: (h, 0, seq0 + b)),
        pl.BlockSpec((seq_len, HEAD_PAD), lambda b, h: (seq0 + b, h)),
        pl.BlockSpec((1, V_ROWS, seq_len), lambda b, h: (h, 0, seq0 + b)),
    ]
    kern = functools.partial(_attn_kernel, tq=tq, tk=tk, nb=nb)
    args = (q_t, k_all, v_t)
    aliases = {}
    if o_prev is not None:
        in_specs.append(pl.BlockSpec(memory_space=pl.ANY))
        args += (o_prev,)
        aliases = {3: 0}
        kern = functools.partial(_attn_kernel_aliased, tq=tq, tk=tk, nb=nb)
    return pl.pallas_call(
        kern,
        grid=(n_seq, N_HEADS),
        in_specs=in_specs,
        out_specs=pl.BlockSpec((V_HEAD_DIM, seq_len), lambda b, h: (h, seq0 + b)),
        out_shape=jax.ShapeDtypeStruct((N_HEADS * V_HEAD_DIM, T), BF16),
        scratch_shapes=[pltpu.VMEM((nb, tk, tq), F32), pltpu.VMEM((nb, 1, tq), F32)],
        input_output_aliases=aliases,
        compiler_params=_cparams("parallel", "parallel"),
        name="mla_attention",
    )(*args)


def _attn_kernel_aliased(q_ref, k_ref, vt_ref, o_prev_ref, o_ref, s_ref, cm_ref, **kw):
    del o_prev_ref
    _attn_kernel(q_ref, k_ref, vt_ref, o_ref, s_ref, cm_ref, **kw)


def _conv_kernel(cur_ref, prev_ref, next_ref, w_ref, b_ref, lng_ref, lnb_ref, og_ref, o_ref,
                 ext_ref, ph_ref, *, seg_tiles, rows):
    i = pl.program_id(0)
    tc = cur_ref.shape[0]
    is_start = _tile_is_boundary(i, seg_tiles, 0)
    is_end = _tile_is_boundary(i, seg_tiles, 1)
    ext_ref[0:HALO, :] = jnp.where(is_start, 0.0, prev_ref[...])
    ext_ref[HALO:HALO + tc, :] = cur_ref[...]
    ext_ref[HALO + tc:2 * HALO + tc, :] = jnp.where(is_end, 0.0, next_ref[...])
    span = ph_ref.shape[1]
    for b in range(SUBLANES):
        first = HALO - CONV_PAD + b
        ph_ref[b] = ext_ref[first:first + span, :]
    bias = b_ref[...]
    lng = lng_ref[...]
    lnb = lnb_ref[...]
    og = og_ref[...]
    for c in range(tc // rows):
        acc = jnp.broadcast_to(bias, (rows, D_CONV))
        for k in range(CONV_WIDTH):
            a, b = divmod(k, SUBLANES)
            first = c * rows + SUBLANES * a
            acc = acc + w_ref[k:k + 1, :] * ph_ref[b, first:first + rows, :]
        mu = jnp.mean(acc, axis=-1, keepdims=True)
        d = acc - mu
        var = jnp.mean(d * d, axis=-1, keepdims=True)
        y = d * lax.rsqrt(var + EPS) * lng + lnb
        y = y * jax.nn.sigmoid(y)
        o_ref[c * rows:(c + 1) * rows, :] = _rms(y, og).astype(o_ref.dtype)


def _tile_is_boundary(i, seg_tiles, end):
    out = None
    for first, n, per in reversed(seg_tiles):
        rel = (i - first + end) % per == 0
        out = rel if out is None else jnp.where(i < first + n, rel, out)
    return out


def _conv_call(hc, lw, segs, tc, rows):
    T = hc.shape[0]
    n_tiles = T // tc
    hb = tc // HALO
    last_halo = T // HALO - 1
    seg_tiles = tuple((off // tc, n * s // tc, s // tc) for off, n, s in segs)
    full = lambda shape: pl.BlockSpec(shape, lambda i: (0,) * len(shape))
    return pl.pallas_call(
        functools.partial(_conv_kernel, seg_tiles=seg_tiles, rows=rows),
        grid=(n_tiles,),
        in_specs=[
            pl.BlockSpec((tc, D_CONV), lambda i: (i, 0)),
            pl.BlockSpec((HALO, D_CONV), lambda i: (jnp.maximum(i * hb - 1, 0), 0)),
            pl.BlockSpec((HALO, D_CONV), lambda i: (jnp.minimum((i + 1) * hb, last_halo), 0)),
            full((CONV_WIDTH, D_CONV)),
            full((1, D_CONV)), full((1, D_CONV)), full((1, D_CONV)), full((1, D_CONV)),
        ],
        out_specs=pl.BlockSpec((tc, D_CONV), lambda i: (i, 0)),
        out_shape=jax.ShapeDtypeStruct((T, D_CONV), BF16),
        scratch_shapes=[pltpu.VMEM((tc + 2 * HALO, D_CONV), F32),
                        pltpu.VMEM((SUBLANES, tc + SUBLANES * ((CONV_WIDTH - 1) // SUBLANES),
                                    D_CONV), F32)],
        compiler_params=_cparams("parallel"),
        name="conv_branch",
    )(hc, hc, hc, lw["conv_dw_w"], lw["conv_dw_b"], lw["conv_ln_g"], lw["conv_ln_b"],
      lw["conv_out_norm_g"])


def _mixer_out(ot_ref, oc_ref, x_ref, ga_ref, wa_ref, wc_ref):
    ot = ot_ref[...].astype(F32)
    r = lax.rsqrt(jnp.mean(ot * ot, axis=0, keepdims=True) + EPS)
    oa = (jnp.transpose(ot * r) * ga_ref[...]).astype(BF16)
    return x_ref[...] + _dot(oa, wa_ref[...]) + _dot(oc_ref[...], wc_ref[...])


def _mixer_specs(tm):
    full2 = lambda shape: pl.BlockSpec(shape, lambda i, j: (0,) * len(shape))
    return [
        pl.BlockSpec((D_ATTN, tm), lambda i, j: (0, i)),
        pl.BlockSpec((tm, D_CONV), lambda i, j: (i, 0)),
        pl.BlockSpec((tm, D_MODEL), lambda i, j: (i, 0)),
        full2((1, D_ATTN)),
        full2((D_ATTN, D_MODEL)),
        full2((D_CONV, D_MODEL)),
    ]


def _mixer_args(o_t, ocn, x, lw):
    return (o_t, ocn, x, lw["attn_out_norm_g"], lw["w_out_a"], lw["w_out_c"])


def _dense_ffn_kernel(ot_ref, oc_ref, x_ref, ga_ref, wa_ref, wc_ref, g_ref, wg_ref, wu_ref,
                      wd_ref, y_ref, h_ref, acc_ref):
    j = pl.program_id(1)

    @pl.when(j == 0)
    def _():
        x = _mixer_out(ot_ref, oc_ref, x_ref, ga_ref, wa_ref, wc_ref)
        h_ref[...] = _rms(x, g_ref[...]).astype(BF16)
        acc_ref[...] = x

    h = h_ref[...]
    gate = _dot(h, wg_ref[...])
    up = _dot(h, wu_ref[...])
    a = (gate * jax.nn.sigmoid(gate) * up).astype(BF16)
    acc_ref[...] += _dot(a, wd_ref[...])

    @pl.when(j == pl.num_programs(1) - 1)
    def _():
        y_ref[...] = acc_ref[...]


def _dense_ffn_call(mixer_args, g, wg, wu, wd, tm, tf):
    T = mixer_args[2].shape[0]
    return pl.pallas_call(
        _dense_ffn_kernel,
        grid=(T // tm, D_FF // tf),
        in_specs=_mixer_specs(tm) + [
            pl.BlockSpec((1, D_MODEL), lambda i, j: (0, 0)),
            pl.BlockSpec((D_MODEL, tf), lambda i, j: (0, j)),
            pl.BlockSpec((D_MODEL, tf), lambda i, j: (0, j)),
            pl.BlockSpec((tf, D_MODEL), lambda i, j: (j, 0)),
        ],
        out_specs=pl.BlockSpec((tm, D_MODEL), lambda i, j: (i, 0)),
        out_shape=jax.ShapeDtypeStruct((T, D_MODEL), F32),
        scratch_shapes=[pltpu.VMEM((tm, D_MODEL), BF16), pltpu.VMEM((tm, D_MODEL), F32)],
        compiler_params=_cparams("parallel", "arbitrary"),
        name="dense_ffn",
    )(*mixer_args, g, wg, wu, wd)


MOE_CHUNK = 9 * BF16_SUBLANES


def _router_top2_t(h_f32, wrt_hi, wrt_lo, br_col):
    h_hi = h_f32.astype(BF16)
    h_lo = (h_f32 - h_hi.astype(F32)).astype(BF16)
    logits = (_dot_nt(wrt_hi, h_hi) + _dot_nt(wrt_hi, h_lo) + _dot_nt(wrt_lo, h_hi))[:N_EXPERTS]
    logits = logits + br_col
    mx = jnp.max(logits, axis=0, keepdims=True)
    ex = jnp.exp(logits - mx)
    probs = ex / jnp.sum(ex, axis=0, keepdims=True)
    row = lax.broadcasted_iota(jnp.int32, probs.shape, 0).astype(F32)
    none = float(N_EXPERTS)
    p1 = jnp.max(probs, axis=0, keepdims=True)
    i1 = jnp.min(jnp.where(probs == p1, row, none), axis=0, keepdims=True)
    sel1 = row == i1
    rest = jnp.where(sel1, -1.0, probs)
    p2 = jnp.max(rest, axis=0, keepdims=True)
    i2 = jnp.min(jnp.where(rest == p2, row, none), axis=0, keepdims=True)
    sel2 = row == i2
    denom = p1 + p2
    gates = jnp.where(sel1, p1 / denom, 0.0) + jnp.where(sel2, p2 / denom, 0.0)
    mask = jnp.where(sel1 | sel2, 1.0, 0.0)
    return gates, mask


def _moe_kernel(ot_ref, oc_ref, x_ref, ga_ref, wa_ref, wc_ref, g_ref, wrh_ref, wrl_ref, br_ref,
                wg_ref, wu_ref, wd_ref, gf_ref, *rest, n_first):
    outs, (h_ref, gates_ref, mask_ref, rank_ref, acc_ref, cnt_ref) = rest[:-6], rest[-6:]
    i = pl.program_id(0)
    e = pl.program_id(1)
    tm = x_ref.shape[0]

    @pl.when(e == 0)
    def _():
        x = _mixer_out(ot_ref, oc_ref, x_ref, ga_ref, wa_ref, wc_ref)
        hf = _rms(x, g_ref[...])
        h_ref[...] = hf.astype(BF16)
        gates, mask = _router_top2_t(hf, wrh_ref[...], wrl_ref[...], br_ref[...])
        gates_ref[...] = gates
        mask_ref[...] = mask
        mask16 = jnp.concatenate([mask, jnp.zeros_like(mask)], axis=0).astype(BF16)
        before = (lax.broadcasted_iota(jnp.int32, (tm, tm), 0)
                  < lax.broadcasted_iota(jnp.int32, (tm, tm), 1))
        rank_ref[...] = _dot(mask16, jnp.where(before, 1.0, 0.0).astype(BF16))[:N_EXPERTS]
        for ex in range(N_EXPERTS):
            cnt_ref[ex] = jnp.sum(mask[ex:ex + 1, :]).astype(jnp.int32)
        acc_ref[...] = x

    n_rows = cnt_ref[e]
    n_chunks = lax.div(n_rows + (MOE_CHUNK - 1), MOE_CHUNK)
    rank_e = rank_ref[pl.ds(e, 1), :]
    routed_e = mask_ref[pl.ds(e, 1), :] > 0.5
    gate_e = gates_ref[pl.ds(e, 1), :]
    slot = lax.broadcasted_iota(jnp.int32, (MOE_CHUNK, tm), 0).astype(F32)

    def chunk(c, _):
        first = (c * MOE_CHUNK).astype(F32)
        sel = (rank_e == slot + first) & routed_e
        sel_b = jnp.where(sel, 1.0, 0.0).astype(BF16)
        xs = _dot(sel_b, h_ref[...]).astype(BF16)
        gate = _dot(xs, wg_ref[0])
        up = _dot(xs, wu_ref[0])
        a = (gate * jax.nn.sigmoid(gate) * up).astype(BF16)
        out = _dot(a, wd_ref[0])
        row_gate = jnp.sum(jnp.where(sel, gate_e, 0.0), axis=1, keepdims=True)
        scaled = (out * row_gate).astype(BF16)
        acc_ref[...] += lax.dot_general(sel_b, scaled, (((0,), (0,)), ((), ())),
                                        preferred_element_type=F32)
        return 0

    lax.fori_loop(0, n_chunks, chunk, 0)

    @pl.when(e == pl.num_programs(1) - 1)
    def _():
        if n_first is None:
            outs[0][...] = acc_ref[...]
        else:
            y = _rms(acc_ref[...], gf_ref[...])

            @pl.when(i < n_first)
            def _():
                outs[0][...] = y

            @pl.when(i >= n_first)
            def _():
                outs[1][...] = y


def _moe_call(mixer_args, g, wrt_hi, wrt_lo, br_col, wg, wu, wd, tm, final=None):
    T = mixer_args[2].shape[0]
    full2 = lambda shape: pl.BlockSpec(shape, lambda i, e: (0,) * len(shape))
    if final is None:
        gf, n_first = g, None
        out_specs = pl.BlockSpec((tm, D_MODEL), lambda i, e: (i, 0))
        out_shape = jax.ShapeDtypeStruct((T, D_MODEL), F32)
    else:
        gf, n_tok_first = final
        n_first = n_tok_first // tm
        out_specs = [pl.BlockSpec((tm, D_MODEL), lambda i, e: (jnp.minimum(i, n_first - 1), 0)),
                     pl.BlockSpec((tm, D_MODEL), lambda i, e: (jnp.maximum(i - n_first, 0), 0))]
        out_shape = [jax.ShapeDtypeStruct((n_tok_first, D_MODEL), F32),
                     jax.ShapeDtypeStruct((T - n_tok_first, D_MODEL), F32)]
    return pl.pallas_call(
        functools.partial(_moe_kernel, n_first=n_first),
        grid=(T // tm, N_EXPERTS),
        in_specs=_mixer_specs(tm) + [
            full2((1, D_MODEL)),
            full2((LANES, D_MODEL)),
            full2((LANES, D_MODEL)),
            full2((N_EXPERTS, 1)),
            pl.BlockSpec((1, D_MODEL, D_EXPERT), lambda i, e: (e, 0, 0)),
            pl.BlockSpec((1, D_MODEL, D_EXPERT), lambda i, e: (e, 0, 0)),
            pl.BlockSpec((1, D_EXPERT, D_MODEL), lambda i, e: (e, 0, 0)),
            full2((1, D_MODEL)),
        ],
        out_specs=out_specs,
        out_shape=out_shape,
        scratch_shapes=[pltpu.VMEM((tm, D_MODEL), BF16),
                        pltpu.VMEM((N_EXPERTS, tm), F32), pltpu.VMEM((N_EXPERTS, tm), F32),
                        pltpu.VMEM((N_EXPERTS, tm), F32),
                        pltpu.VMEM((tm, D_MODEL), F32),
                        pltpu.SMEM((N_EXPERTS,), jnp.int32)],
        compiler_params=_cparams("arbitrary", "arbitrary"),
        name="moe_ffn",
    )(*mixer_args, g, wrt_hi, wrt_lo, br_col, wg, wu, wd, gf)


def _rope_tables(max_len):
    pos = jnp.arange(max_len, dtype=F32)
    freqs = ROPE_THETA ** (-(jnp.arange(0, QK_ROPE_DIM, 2, dtype=F32) / QK_ROPE_DIM))
    ang = pos[:, None] * freqs[None, :]
    ang = jnp.concatenate([ang, ang], axis=-1)
    cos, sin = jnp.cos(ang), jnp.sin(ang)
    pad = lambda t: jnp.pad(t, ((0, 0), (QK_NOPE_DIM, HEAD_PAD - QK_HEAD_DIM)))
    return {"cosk": pad(cos), "sink": pad(sin),
            "cosq_t": (cos * Q_SCALE).T, "sinq_t": (sin * Q_SCALE).T}


def _rot_cols(w):
    half = QK_ROPE_DIM // 2
    return jnp.concatenate([-w[..., half:], w[..., :half]], axis=-1)


def _layer_weights(i, p):
    row = lambda v: v.reshape(1, -1)
    w_in = p["w_in"][i]
    w_kr = w_in[:, OFF_KR:]
    place = lambda w: jnp.pad(w, ((0, 0), (QK_NOPE_DIM, HEAD_PAD - QK_HEAD_DIM)))
    w_in_ext = jnp.concatenate([w_in[:, :OFF_KR], place(w_kr), place(_rot_cols(w_kr))], axis=1)
    w_uq = p["w_uq"][i].reshape(Q_LORA_RANK, N_HEADS, QK_HEAD_DIM)
    wq_pad = jnp.pad(w_uq, ((0, 0), (0, 0), (0, HEAD_PAD - QK_HEAD_DIM)))
    wq_t = wq_pad.reshape(Q_LORA_RANK, N_HEADS * HEAD_PAD).T
    wqrot_t = _rot_cols(w_uq[..., QK_NOPE_DIM:]).reshape(Q_LORA_RANK, N_HEADS * QK_ROPE_DIM).T
    w_ukv = p["w_ukv"][i].reshape(KV_LORA_RANK, N_HEADS, QK_NOPE_DIM + V_HEAD_DIM)
    wk = jnp.pad(w_ukv[..., :QK_NOPE_DIM], ((0, 0), (0, 0), (0, HEAD_PAD - QK_NOPE_DIM)))
    wk = wk.reshape(KV_LORA_RANK, N_HEADS * HEAD_PAD)
    wv_t = w_ukv[..., QK_NOPE_DIM:].reshape(KV_LORA_RANK, N_HEADS * V_HEAD_DIM).T
    w_out = p["w_out"][i]
    return {
        "attn_norm_g": row(p["attn_norm_g"][i]),
        "w_in": w_in_ext.astype(BF16),
        "q_norm_g": row(p["q_norm_g"][i]),
        "wq_t": wq_t.astype(BF16),
        "wqrot_t": wqrot_t.astype(BF16),
        "kv_norm_g": row(p["kv_norm_g"][i]),
        "wk": wk.astype(BF16),
        "wv_t": wv_t.astype(BF16),
        "conv_dw_w": p["conv_dw_w"][i].reshape(CONV_WIDTH, D_CONV),
        "conv_dw_b": row(p["conv_dw_b"][i]),
        "conv_ln_g": row(p["conv_ln_g"][i]),
        "conv_ln_b": row(p["conv_ln_b"][i]),
        "conv_out_norm_g": row(p["conv_out_norm_g"][i]),
        "attn_out_norm_g": row(p["attn_out_norm_g"][i]),
        "w_out_a": w_out[:D_ATTN].astype(BF16),
        "w_out_c": w_out[D_ATTN:].astype(BF16),
        "ffn_norm_g": row(p["ffn_norm_g"][i]),
    }


def _split_hi_lo(w):
    hi = w.astype(BF16)
    lo = (w - hi.astype(F32)).astype(BF16)
    return hi, lo


def _tile_sizes(segs):
    s_min = min(s for _, _, s in segs)
    return {
        "tm": min(512, s_min),
        "tc": min(512, s_min),
        "rows": 64,
        "tq": min(512, s_min),
        "tk": min(512, s_min),
        "nb": 4,
        "tf": D_FF // 2,
        "tme": min(1024, s_min),
    }


def kernel(x_prompt, x_sample, attn_norm_g, w_in, q_norm_g, w_uq, kv_norm_g, w_ukv, conv_dw_w,
           conv_dw_b, conv_ln_g, conv_ln_b, attn_out_norm_g, conv_out_norm_g, w_out, ffn_norm_g,
           dense_w_gate, dense_w_up, dense_w_down, moe_w_router, moe_b_router, moe_w_gate,
           moe_w_up, moe_w_down, final_norm_g):
    p = dict(attn_norm_g=attn_norm_g, w_in=w_in, q_norm_g=q_norm_g, w_uq=w_uq,
             kv_norm_g=kv_norm_g, w_ukv=w_ukv, conv_dw_w=conv_dw_w, conv_dw_b=conv_dw_b,
             conv_ln_g=conv_ln_g, conv_ln_b=conv_ln_b, attn_out_norm_g=attn_out_norm_g,
             conv_out_norm_g=conv_out_norm_g, w_out=w_out, ffn_norm_g=ffn_norm_g)
    bp, sp, _ = x_prompt.shape
    bs, ss, _ = x_sample.shape
    n_p, n_s = bp * sp, bs * ss
    segs = ((0, bp, sp), (n_p, bs, ss))
    assert n_p % ss == 0 and n_p % sp == 0
    ts = _tile_sizes(segs)
    tabs = _rope_tables(max(sp, ss))
    x = jnp.concatenate([x_prompt.reshape(n_p, D_MODEL), x_sample.reshape(n_s, D_MODEL)], axis=0)

    assert DEPTH % 2 == 0 and n_p % ts["tme"] == 0
    for i in range(DEPTH):
        lw = _layer_weights(i, p)
        hc, q_t, k_all, v_t = _pre_call(x, lw, tabs, segs, ts["tm"])
        o_t = None
        for off, n, s in segs:
            o_t = _attn_call(q_t, k_all, v_t, o_t, off, n, s, ts["tq"], ts["tk"],
                             math.gcd(ts["nb"], s // ts["tk"]))
        ocn = _conv_call(hc, lw, segs, ts["tc"], ts["rows"])
        mixer_args = _mixer_args(o_t, ocn, x, lw)
        j = i // 2
        if i % 2 == 0:
            x = _dense_ffn_call(mixer_args, lw["ffn_norm_g"], dense_w_gate[j].astype(BF16),
                                dense_w_up[j].astype(BF16), dense_w_down[j].astype(BF16),
                                ts["tm"], ts["tf"])
        else:
            wrt = jnp.pad(moe_w_router[j].T, ((0, LANES - N_EXPERTS), (0, 0)))
            wrt_hi, wrt_lo = _split_hi_lo(wrt)
            final = (final_norm_g.reshape(1, D_MODEL), n_p) if i == DEPTH - 1 else None
            x = _moe_call(mixer_args, lw["ffn_norm_g"], wrt_hi, wrt_lo,
                          moe_b_router[j].reshape(N_EXPERTS, 1),
                          moe_w_gate[j].astype(BF16), moe_w_up[j].astype(BF16),
                          moe_w_down[j].astype(BF16), ts["tme"], final)

    y_p, y_s = x
    return (y_p.reshape(bp, sp, D_MODEL), y_s.reshape(bs, ss, D_MODEL))
```

```python
import functools
import math

import jax
import jax.numpy as jnp
import numpy as np
from jax import lax
from jax.experimental import pallas as pl
from jax.experimental.pallas import tpu as pltpu

D_MODEL = 1024
DEPTH = 4
D_ATTN = 512
D_CONV = 512
N_HEADS = 8
QK_NOPE_DIM = 64
QK_ROPE_DIM = 32
V_HEAD_DIM = 64
QK_HEAD_DIM = QK_NOPE_DIM + QK_ROPE_DIM
Q_LORA_RANK = 256
KV_LORA_RANK = 128
ROPE_THETA = 10000.0
CONV_WIDTH = 31
CONV_PAD = (CONV_WIDTH - 1) // 2
D_FF = 2816
N_EXPERTS = 8
D_EXPERT = 1024
EPS = 1e-6
OFF_CQ = 2 * D_CONV
OFF_CKV = OFF_CQ + Q_LORA_RANK
OFF_KR = OFF_CKV + KV_LORA_RANK

LANES = 128
SUBLANES = 8
HEAD_PAD = LANES
HALO = 16
BF16_SUBLANES = 16
V_ROWS = V_HEAD_DIM + BF16_SUBLANES
VMEM_LIMIT_BYTES = 56 * 1024 * 1024

BF16 = jnp.bfloat16
F32 = jnp.float32
NEG_BIG = -1e30
Q_SCALE = (QK_HEAD_DIM ** -0.5) * math.log2(math.e)


def _cparams(*sem):
    return pltpu.CompilerParams(dimension_semantics=sem, vmem_limit_bytes=VMEM_LIMIT_BYTES)


def _rms(x, g):
    return x * lax.rsqrt(jnp.mean(x * x, axis=-1, keepdims=True) + EPS) * g


def _dot(a, b):
    return jnp.dot(a, b, preferred_element_type=F32)


def _dot_nt(a, b):
    return lax.dot_general(a, b, (((1,), (1,)), ((), ())), preferred_element_type=F32)


def _pre_kernel(x_ref, g_ref, win_ref, qg_ref, wqt_ref, wqrt_ref, kvg_ref, wk_ref, wvt_ref,
                cosk_ref, sink_ref, cosq_ref, sinq_ref,
                hc_ref, q_ref, k_ref, vt_ref):
    x = x_ref[...]
    h = _rms(x, g_ref[...]).astype(BF16)
    z = _dot(h, win_ref[...])
    hc_ref[...] = z[:, :D_CONV] * jax.nn.sigmoid(z[:, D_CONV:2 * D_CONV])
    cqn = _rms(z[:, OFF_CQ:OFF_CKV], qg_ref[...]).astype(BF16)
    ckvn = _rms(z[:, OFF_CKV:OFF_KR], kvg_ref[...]).astype(BF16)
    kr = (z[:, OFF_KR:OFF_KR + LANES] * cosk_ref[...]
          + z[:, OFF_KR + LANES:OFF_KR + 2 * LANES] * sink_ref[...])
    kall = _dot(ckvn, wk_ref[...])
    for hd in range(N_HEADS):
        sl = slice(hd * HEAD_PAD, (hd + 1) * HEAD_PAD)
        k_ref[:, sl] = (kall[:, sl] + kr).astype(BF16)
    vt = _dot_nt(wvt_ref[...], ckvn)
    ones = jnp.ones((V_ROWS - V_HEAD_DIM, x.shape[0]), BF16)
    for hd in range(N_HEADS):
        vt_ref[hd, 0:V_HEAD_DIM, :] = vt[hd * V_HEAD_DIM:(hd + 1) * V_HEAD_DIM].astype(BF16)
        vt_ref[hd, V_HEAD_DIM:V_ROWS, :] = ones
    qt = _dot_nt(wqt_ref[...], cqn)
    qrt = _dot_nt(wqrt_ref[...], cqn)
    cq = cosq_ref[...]
    sq = sinq_ref[...]
    zeros = jnp.zeros((HEAD_PAD - QK_HEAD_DIM, x.shape[0]), BF16)
    for hd in range(N_HEADS):
        base = hd * HEAD_PAD
        q_ref[hd, 0:QK_NOPE_DIM, :] = (qt[base:base + QK_NOPE_DIM] * Q_SCALE).astype(BF16)
        rope = (qt[base + QK_NOPE_DIM:base + QK_HEAD_DIM] * cq
                + qrt[hd * QK_ROPE_DIM:(hd + 1) * QK_ROPE_DIM] * sq)
        q_ref[hd, QK_NOPE_DIM:QK_HEAD_DIM, :] = rope.astype(BF16)
        q_ref[hd, QK_HEAD_DIM:HEAD_PAD, :] = zeros


def _pre_call(x, lw, tabs, segs, tm):
    T = x.shape[0]
    n_tiles = T // tm
    pos_map = _pos_block_map(segs, tm)
    full = lambda shape: pl.BlockSpec(shape, lambda i: (0,) * len(shape))
    in_cols = lw["w_in"].shape[1]
    return pl.pallas_call(
        _pre_kernel,
        grid=(n_tiles,),
        in_specs=[
            pl.BlockSpec((tm, D_MODEL), lambda i: (i, 0)),
            full((1, D_MODEL)),
            full((D_MODEL, in_cols)),
            full((1, Q_LORA_RANK)),
            full((N_HEADS * HEAD_PAD, Q_LORA_RANK)),
            full((N_HEADS * QK_ROPE_DIM, Q_LORA_RANK)),
            full((1, KV_LORA_RANK)),
            full((KV_LORA_RANK, N_HEADS * HEAD_PAD)),
            full((N_HEADS * V_HEAD_DIM, KV_LORA_RANK)),
            pl.BlockSpec((tm, LANES), lambda i: (pos_map(i), 0)),
            pl.BlockSpec((tm, LANES), lambda i: (pos_map(i), 0)),
            pl.BlockSpec((QK_ROPE_DIM, tm), lambda i: (0, pos_map(i))),
            pl.BlockSpec((QK_ROPE_DIM, tm), lambda i: (0, pos_map(i))),
        ],
        out_specs=[
            pl.BlockSpec((tm, D_CONV), lambda i: (i, 0)),
            pl.BlockSpec((N_HEADS, HEAD_PAD, tm), lambda i: (0, 0, i)),
            pl.BlockSpec((tm, N_HEADS * HEAD_PAD), lambda i: (i, 0)),
            pl.BlockSpec((N_HEADS, V_ROWS, tm), lambda i: (0, 0, i)),
        ],
        out_shape=[
            jax.ShapeDtypeStruct((T, D_CONV), F32),
            jax.ShapeDtypeStruct((N_HEADS, HEAD_PAD, T), BF16),
            jax.ShapeDtypeStruct((T, N_HEADS * HEAD_PAD), BF16),
            jax.ShapeDtypeStruct((N_HEADS, V_ROWS, T), BF16),
        ],
        compiler_params=_cparams("parallel"),
        name="layer_front",
    )(x, lw["attn_norm_g"], lw["w_in"], lw["q_norm_g"], lw["wq_t"], lw["wqrot_t"],
      lw["kv_norm_g"], lw["wk"], lw["wv_t"], tabs["cosk"], tabs["sink"], tabs["cosq_t"],
      tabs["sinq_t"])


def _pos_block_map(segs, tile):
    def pos_map(i):
        out = None
        for off, n_seq, seq_len in reversed(segs):
            blk = (i - off // tile) % (seq_len // tile)
            out = blk if out is None else jnp.where(i < (off + n_seq * seq_len) // tile, blk, out)
        return out
    return pos_map


def _attn_kernel(q_ref, k_ref, vt_ref, o_ref, s_ref, cm_ref, *, tq, tk, nb):
    seq = k_ref.shape[0]
    n_kv = seq // tk
    n_body = n_kv // nb
    n_q = seq // tq
    look = nb - 1

    def q_tile(qi):
        return q_ref[0, :, pl.ds(pl.multiple_of(qi * tq, tq), tq)]

    def stage_a(q, j, slot):
        start = pl.multiple_of(j * tk, tk)
        s = _dot(k_ref[pl.ds(start, tk), :], q)
        s_ref[slot] = s
        cm_ref[slot] = jnp.max(s, axis=0, keepdims=True)

    def stage_bc(j, slot, m, acc):
        m_new = jnp.maximum(m, cm_ref[slot])
        alpha = jnp.exp2(m - m_new)
        p = jnp.exp2(s_ref[slot] - m_new).astype(BF16)
        start = pl.multiple_of(j * tk, tk)
        acc = alpha * acc + _dot(vt_ref[0, :, pl.ds(start, tk)], p)
        return m_new, acc

    def body(q_cur, q_next, j0, carry, last=False):
        m, acc = carry
        for i in range(nb):
            if last and i >= 1:
                stage_a(q_next, i - 1, (i + look) % nb)
            else:
                stage_a(q_cur, j0 + i + look, (i + look) % nb)
            m, acc = stage_bc(j0 + i, i, m, acc)
        return m, acc

    q_first = q_tile(0)
    for j in range(look):
        stage_a(q_first, j, j)

    def per_query_tile(qi, _):
        q_cur = q_tile(qi)
        q_next = q_tile(jnp.minimum(qi + 1, n_q - 1))
        carry = (jnp.full((1, tq), NEG_BIG, F32), jnp.zeros((V_ROWS, tq), F32))
        if n_body > 1:
            carry = lax.fori_loop(0, n_body - 1,
                                  lambda jb, c: body(q_cur, q_next, jb * nb, c), carry)
        _, acc = body(q_cur, q_next, (n_body - 1) * nb, carry, last=True)
        out = acc[:V_HEAD_DIM] / acc[V_HEAD_DIM:V_HEAD_DIM + 1]
        o_ref[:, pl.ds(pl.multiple_of(qi * tq, tq), tq)] = out.astype(o_ref.dtype)
        return 0

    lax.fori_loop(0, n_q, per_query_tile, 0)


def _attn_call(q_t, k_all, v_t, o_prev, tok_off, n_seq, seq_len, tq, tk, nb):
    assert (seq_len // tk) % nb == 0 and seq_len % tq == 0
    T = k_all.shape[0]
    seq0 = tok_off // seq_len
    in_specs = [
        pl.BlockSpec((1, HEAD_PAD, seq_len), lambda b, h: (h, 0, seq0 + b)),
        pl.BlockSpec((seq_len, HEAD_PAD), lambda b, h: (seq0 + b, h)),
        pl.BlockSpec((1, V_ROWS, seq_len), lambda b, h: (h, 0, seq0 + b)),
    ]
    kern = functools.partial(_attn_kernel, tq=tq, tk=tk, nb=nb)
    args = (q_t, k_all, v_t)
    aliases = {}
    if o_prev is not None:
        in_specs.append(pl.BlockSpec(memory_space=pl.ANY))
        args += (o_prev,)
        aliases = {3: 0}
        kern = functools.partial(_attn_kernel_aliased, tq=tq, tk=tk, nb=nb)
    return pl.pallas_call(
        kern,
        grid=(n_seq, N_HEADS),
        in_specs=in_specs,
        out_specs=pl.BlockSpec((V_HEAD_DIM, seq_len), lambda b, h: (h, seq0 + b)),
        out_shape=jax.ShapeDtypeStruct((N_HEADS * V_HEAD_DIM, T), BF16),
        scratch_shapes=[pltpu.VMEM((nb, tk, tq), F32), pltpu.VMEM((nb, 1, tq), F32)],
        input_output_aliases=aliases,
        compiler_params=_cparams("parallel", "parallel"),
        name="mla_attention",
    )(*args)


def _attn_kernel_aliased(q_ref, k_ref, vt_ref, o_prev_ref, o_ref, s_ref, cm_ref, **kw):
    del o_prev_ref
    _attn_kernel(q_ref, k_ref, vt_ref, o_ref, s_ref, cm_ref, **kw)


def _conv_kernel(cur_ref, prev_ref, next_ref, w_ref, b_ref, lng_ref, lnb_ref, og_ref, o_ref,
                 ext_ref, ph_ref, *, seg_tiles, rows):
    i = pl.program_id(0)
    tc = cur_ref.shape[0]
    is_start = _tile_is_boundary(i, seg_tiles, 0)
    is_end = _tile_is_boundary(i, seg_tiles, 1)
    ext_ref[0:HALO, :] = jnp.where(is_start, 0.0, prev_ref[...])
    ext_ref[HALO:HALO + tc, :] = cur_ref[...]
    ext_ref[HALO + tc:2 * HALO + tc, :] = jnp.where(is_end, 0.0, next_ref[...])
    span = ph_ref.shape[1]
    for b in range(SUBLANES):
        first = HALO - CONV_PAD + b
        ph_ref[b] = ext_ref[first:first + span, :]
    bias = b_ref[...]
    lng = lng_ref[...]
    lnb = lnb_ref[...]
    og = og_ref[...]
    for c in range(tc // rows):
        acc = jnp.broadcast_to(bias, (rows, D_CONV))
        for k in range(CONV_WIDTH):
            a, b = divmod(k, SUBLANES)
            first = c * rows + SUBLANES * a
            acc = acc + w_ref[k:k + 1, :] * ph_ref[b, first:first + rows, :]
        mu = jnp.mean(acc, axis=-1, keepdims=True)
        d = acc - mu
        var = jnp.mean(d * d, axis=-1, keepdims=True)
        y = d * lax.rsqrt(var + EPS) * lng + lnb
        y = y * jax.nn.sigmoid(y)
        o_ref[c * rows:(c + 1) * rows, :] = _rms(y, og).astype(o_ref.dtype)


def _tile_is_boundary(i, seg_tiles, end):
    out = None
    for first, n, per in reversed(seg_tiles):
        rel = (i - first + end) % per == 0
        out = rel if out is None else jnp.where(i < first + n, rel, out)
    return out


def _conv_call(hc, lw, segs, tc, rows):
    T = hc.shape[0]
    n_tiles = T // tc
    hb = tc // HALO
    last_halo = T // HALO - 1
    seg_tiles = tuple((off // tc, n * s // tc, s // tc) for off, n, s in segs)
    full = lambda shape: pl.BlockSpec(shape, lambda i: (0,) * len(shape))
    return pl.pallas_call(
        functools.partial(_conv_kernel, seg_tiles=seg_tiles, rows=rows),
        grid=(n_tiles,),
        in_specs=[
            pl.BlockSpec((tc, D_CONV), lambda i: (i, 0)),
            pl.BlockSpec((HALO, D_CONV), lambda i: (jnp.maximum(i * hb - 1, 0), 0)),
            pl.BlockSpec((HALO, D_CONV), lambda i: (jnp.minimum((i + 1) * hb, last_halo), 0)),
            full((CONV_WIDTH, D_CONV)),
            full((1, D_CONV)), full((1, D_CONV)), full((1, D_CONV)), full((1, D_CONV)),
        ],
        out_specs=pl.BlockSpec((tc, D_CONV), lambda i: (i, 0)),
        out_shape=jax.ShapeDtypeStruct((T, D_CONV), BF16),
        scratch_shapes=[pltpu.VMEM((tc + 2 * HALO, D_CONV), F32),
                        pltpu.VMEM((SUBLANES, tc + SUBLANES * ((CONV_WIDTH - 1) // SUBLANES),
                                    D_CONV), F32)],
        compiler_params=_cparams("parallel"),
        name="conv_branch",
    )(hc, hc, hc, lw["conv_dw_w"], lw["conv_dw_b"], lw["conv_ln_g"], lw["conv_ln_b"],
      lw["conv_out_norm_g"])


def _mixer_out(ot_ref, oc_ref, x_ref, ga_ref, wa_ref, wc_ref):
    ot = ot_ref[...].astype(F32)
    r = lax.rsqrt(jnp.mean(ot * ot, axis=0, keepdims=True) + EPS)
    oa = (jnp.transpose(ot * r) * ga_ref[...]).astype(BF16)
    return x_ref[...] + _dot(oa, wa_ref[...]) + _dot(oc_ref[...], wc_ref[...])


def _mixer_specs(tm):
    full2 = lambda shape: pl.BlockSpec(shape, lambda i, j: (0,) * len(shape))
    return [
        pl.BlockSpec((D_ATTN, tm), lambda i, j: (0, i)),
        pl.BlockSpec((tm, D_CONV), lambda i, j: (i, 0)),
        pl.BlockSpec((tm, D_MODEL), lambda i, j: (i, 0)),
        full2((1, D_ATTN)),
        full2((D_ATTN, D_MODEL)),
        full2((D_CONV, D_MODEL)),
    ]


def _mixer_args(o_t, ocn, x, lw):
    return (o_t, ocn, x, lw["attn_out_norm_g"], lw["w_out_a"], lw["w_out_c"])


def _dense_ffn_kernel(ot_ref, oc_ref, x_ref, ga_ref, wa_ref, wc_ref, g_ref, wg_ref, wu_ref,
                      wd_ref, y_ref, h_ref, acc_ref):
    j = pl.program_id(1)

    @pl.when(j == 0)
    def _():
        x = _mixer_out(ot_ref, oc_ref, x_ref, ga_ref, wa_ref, wc_ref)
        h_ref[...] = _rms(x, g_ref[...]).astype(BF16)
        acc_ref[...] = x

    h = h_ref[...]
    gate = _dot(h, wg_ref[...])
    up = _dot(h, wu_ref[...])
    a = (gate * jax.nn.sigmoid(gate) * up).astype(BF16)
    acc_ref[...] += _dot(a, wd_ref[...])

    @pl.when(j == pl.num_programs(1) - 1)
    def _():
        y_ref[...] = acc_ref[...]


def _dense_ffn_call(mixer_args, g, wg, wu, wd, tm, tf):
    T = mixer_args[2].shape[0]
    return pl.pallas_call(
        _dense_ffn_kernel,
        grid=(T // tm, D_FF // tf),
        in_specs=_mixer_specs(tm) + [
            pl.BlockSpec((1, D_MODEL), lambda i, j: (0, 0)),
            pl.BlockSpec((D_MODEL, tf), lambda i, j: (0, j)),
            pl.BlockSpec((D_MODEL, tf), lambda i, j: (0, j)),
            pl.BlockSpec((tf, D_MODEL), lambda i, j: (j, 0)),
        ],
        out_specs=pl.BlockSpec((tm, D_MODEL), lambda i, j: (i, 0)),
        out_shape=jax.ShapeDtypeStruct((T, D_MODEL), F32),
        scratch_shapes=[pltpu.VMEM((tm, D_MODEL), BF16), pltpu.VMEM((tm, D_MODEL), F32)],
        compiler_params=_cparams("parallel", "arbitrary"),
        name="dense_ffn",
    )(*mixer_args, g, wg, wu, wd)


MOE_SUB = 512
MOE_CHUNK = 9 * BF16_SUBLANES


def _router_top2_t(h_f32, wrt_hi, wrt_lo, br_col):
    h_hi = h_f32.astype(BF16)
    h_lo = (h_f32 - h_hi.astype(F32)).astype(BF16)
    logits = (_dot_nt(wrt_hi, h_hi) + _dot_nt(wrt_hi, h_lo) + _dot_nt(wrt_lo, h_hi))[:N_EXPERTS]
    logits = logits + br_col
    mx = jnp.max(logits, axis=0, keepdims=True)
    ex = jnp.exp(logits - mx)
    probs = ex / jnp.sum(ex, axis=0, keepdims=True)
    row = lax.broadcasted_iota(jnp.int32, probs.shape, 0).astype(F32)
    none = float(N_EXPERTS)
    p1 = jnp.max(probs, axis=0, keepdims=True)
    i1 = jnp.min(jnp.where(probs == p1, row, none), axis=0, keepdims=True)
    sel1 = row == i1
    rest = jnp.where(sel1, -1.0, probs)
    p2 = jnp.max(rest, axis=0, keepdims=True)
    i2 = jnp.min(jnp.where(rest == p2, row, none), axis=0, keepdims=True)
    sel2 = row == i2
    denom = p1 + p2
    gates = jnp.where(sel1, p1 / denom, 0.0) + jnp.where(sel2, p2 / denom, 0.0)
    mask = jnp.where(sel1 | sel2, 1.0, 0.0)
    return gates, mask


def _moe_kernel(ot_ref, oc_ref, x_ref, ga_ref, wa_ref, wc_ref, g_ref, wrh_ref, wrl_ref, br_ref,
                wg_ref, wu_ref, wd_ref, gf_ref, *rest, n_first):
    outs, (h_ref, gates_ref, mask_ref, rank_ref, acc_ref, cnt_ref) = rest[:-6], rest[-6:]
    i = pl.program_id(0)
    e = pl.program_id(1)
    tm = x_ref.shape[0]
    ts = min(MOE_SUB, tm)

    @pl.when(e == 0)
    def _():
        x = _mixer_out(ot_ref, oc_ref, x_ref, ga_ref, wa_ref, wc_ref)
        hf = _rms(x, g_ref[...])
        h_ref[...] = hf.astype(BF16)
        gates, mask = _router_top2_t(hf, wrh_ref[...], wrl_ref[...], br_ref[...])
        gates_ref[...] = gates
        mask_ref[...] = mask
        mask16 = jnp.concatenate([mask, jnp.zeros_like(mask)], axis=0).astype(BF16)
        before = (lax.broadcasted_iota(jnp.int32, (ts, ts), 0)
                  < lax.broadcasted_iota(jnp.int32, (ts, ts), 1))
        before = jnp.where(before, 1.0, 0.0).astype(BF16)
        for sub in range(tm // ts):
            cols = slice(sub * ts, (sub + 1) * ts)
            rank_ref[:, cols] = _dot(mask16[:, cols], before)[:N_EXPERTS]
            for ex in range(N_EXPERTS):
                cnt_ref[ex * (tm // ts) + sub] = jnp.sum(mask[ex:ex + 1, cols]).astype(jnp.int32)
        acc_ref[...] = x

    slot = lax.broadcasted_iota(jnp.int32, (MOE_CHUNK, ts), 0).astype(F32)
    for sub in range(tm // ts):
        cols = slice(sub * ts, (sub + 1) * ts)
        n_rows = cnt_ref[e * (tm // ts) + sub]
        n_chunks = lax.div(n_rows + (MOE_CHUNK - 1), MOE_CHUNK)
        rank_e = rank_ref[pl.ds(e, 1), cols]
        routed_e = mask_ref[pl.ds(e, 1), cols] > 0.5
        gate_e = gates_ref[pl.ds(e, 1), cols]

        def chunk(c, _, cols=cols, rank_e=rank_e, routed_e=routed_e, gate_e=gate_e):
            first = (c * MOE_CHUNK).astype(F32)
            sel = (rank_e == slot + first) & routed_e
            sel_b = jnp.where(sel, 1.0, 0.0).astype(BF16)
            xs = _dot(sel_b, h_ref[cols, :]).astype(BF16)
            gate = _dot(xs, wg_ref[0])
            up = _dot(xs, wu_ref[0])
            a = (gate * jax.nn.sigmoid(gate) * up).astype(BF16)
            out = _dot(a, wd_ref[0])
            row_gate = jnp.sum(jnp.where(sel, gate_e, 0.0), axis=1, keepdims=True)
            scaled = (out * row_gate).astype(BF16)
            acc_ref[cols, :] += lax.dot_general(sel_b, scaled, (((0,), (0,)), ((), ())),
                                                preferred_element_type=F32)
            return 0

        lax.fori_loop(0, n_chunks, chunk, 0)

    @pl.when(e == pl.num_programs(1) - 1)
    def _():
        if n_first is None:
            outs[0][...] = acc_ref[...]
        else:
            y = _rms(acc_ref[...], gf_ref[...])

            @pl.when(i < n_first)
            def _():
                outs[0][...] = y

            @pl.when(i >= n_first)
            def _():
                outs[1][...] = y


def _moe_call(mixer_args, g, wrt_hi, wrt_lo, br_col, wg, wu, wd, tm, final=None):
    T = mixer_args[2].shape[0]
    full2 = lambda shape: pl.BlockSpec(shape, lambda i, e: (0,) * len(shape))
    if final is None:
        gf, n_first = g, None
        out_specs = pl.BlockSpec((tm, D_MODEL), lambda i, e: (i, 0))
        out_shape = jax.ShapeDtypeStruct((T, D_MODEL), F32)
    else:
        gf, n_tok_first = final
        n_first = n_tok_first // tm
        out_specs = [pl.BlockSpec((tm, D_MODEL), lambda i, e: (jnp.minimum(i, n_first - 1), 0)),
                     pl.BlockSpec((tm, D_MODEL), lambda i, e: (jnp.maximum(i - n_first, 0), 0))]
        out_shape = [jax.ShapeDtypeStruct((n_tok_first, D_MODEL), F32),
                     jax.ShapeDtypeStruct((T - n_tok_first, D_MODEL), F32)]
    return pl.pallas_call(
        functools.partial(_moe_kernel, n_first=n_first),
        grid=(T // tm, N_EXPERTS),
        in_specs=_mixer_specs(tm) + [
            full2((1, D_MODEL)),
            full2((LANES, D_MODEL)),
            full2((LANES, D_MODEL)),
            full2((N_EXPERTS, 1)),
            pl.BlockSpec((1, D_MODEL, D_EXPERT), lambda i, e: (e, 0, 0)),
            pl.BlockSpec((1, D_MODEL, D_EXPERT), lambda i, e: (e, 0, 0)),
            pl.BlockSpec((1, D_EXPERT, D_MODEL), lambda i, e: (e, 0, 0)),
            full2((1, D_MODEL)),
        ],
        out_specs=out_specs,
        out_shape=out_shape,
        scratch_shapes=[pltpu.VMEM((tm, D_MODEL), BF16),
                        pltpu.VMEM((N_EXPERTS, tm), F32), pltpu.VMEM((N_EXPERTS, tm), F32),
                        pltpu.VMEM((N_EXPERTS, tm), F32),
                        pltpu.VMEM((tm, D_MODEL), F32),
                        pltpu.SMEM((N_EXPERTS * (tm // min(MOE_SUB, tm)),), jnp.int32)],
        compiler_params=_cparams("arbitrary", "arbitrary"),
        name="moe_ffn",
    )(*mixer_args, g, wrt_hi, wrt_lo, br_col, wg, wu, wd, gf)


def _rope_tables(max_len):
    pos = jnp.arange(max_len, dtype=F32)
    freqs = ROPE_THETA ** (-(jnp.arange(0, QK_ROPE_DIM, 2, dtype=F32) / QK_ROPE_DIM))
    ang = pos[:, None] * freqs[None, :]
    ang = jnp.concatenate([ang, ang], axis=-1)
    cos, sin = jnp.cos(ang), jnp.sin(ang)
    pad = lambda t: jnp.pad(t, ((0, 0), (QK_NOPE_DIM, HEAD_PAD - QK_HEAD_DIM)))
    return {"cosk": pad(cos), "sink": pad(sin),
            "cosq_t": (cos * Q_SCALE).T, "sinq_t": (sin * Q_SCALE).T}


def _rot_cols(w):
    half = QK_ROPE_DIM // 2
    return jnp.concatenate([-w[..., half:], w[..., :half]], axis=-1)


def _layer_weights(i, p):
    row = lambda v: v.reshape(1, -1)
    w_in = p["w_in"][i]
    w_kr = w_in[:, OFF_KR:]
    place = lambda w: jnp.pad(w, ((0, 0), (QK_NOPE_DIM, HEAD_PAD - QK_HEAD_DIM)))
    w_in_ext = jnp.concatenate([w_in[:, :OFF_KR], place(w_kr), place(_rot_cols(w_kr))], axis=1)
    w_uq = p["w_uq"][i].reshape(Q_LORA_RANK, N_HEADS, QK_HEAD_DIM)
    wq_pad = jnp.pad(w_uq, ((0, 0), (0, 0), (0, HEAD_PAD - QK_HEAD_DIM)))
    wq_t = wq_pad.reshape(Q_LORA_RANK, N_HEADS * HEAD_PAD).T
    wqrot_t = _rot_cols(w_uq[..., QK_NOPE_DIM:]).reshape(Q_LORA_RANK, N_HEADS * QK_ROPE_DIM).T
    w_ukv = p["w_ukv"][i].reshape(KV_LORA_RANK, N_HEADS, QK_NOPE_DIM + V_HEAD_DIM)
    wk = jnp.pad(w_ukv[..., :QK_NOPE_DIM], ((0, 0), (0, 0), (0, HEAD_PAD - QK_NOPE_DIM)))
    wk = wk.reshape(KV_LORA_RANK, N_HEADS * HEAD_PAD)
    wv_t = w_ukv[..., QK_NOPE_DIM:].reshape(KV_LORA_RANK, N_HEADS * V_HEAD_DIM).T
    w_out = p["w_out"][i]
    return {
        "attn_norm_g": row(p["attn_norm_g"][i]),
        "w_in": w_in_ext.astype(BF16),
        "q_norm_g": row(p["q_norm_g"][i]),
        "wq_t": wq_t.astype(BF16),
        "wqrot_t": wqrot_t.astype(BF16),
        "kv_norm_g": row(p["kv_norm_g"][i]),
        "wk": wk.astype(BF16),
        "wv_t": wv_t.astype(BF16),
        "conv_dw_w": p["conv_dw_w"][i].reshape(CONV_WIDTH, D_CONV),
        "conv_dw_b": row(p["conv_dw_b"][i]),
        "conv_ln_g": row(p["conv_ln_g"][i]),
        "conv_ln_b": row(p["conv_ln_b"][i]),
        "conv_out_norm_g": row(p["conv_out_norm_g"][i]),
        "attn_out_norm_g": row(p["attn_out_norm_g"][i]),
        "w_out_a": w_out[:D_ATTN].astype(BF16),
        "w_out_c": w_out[D_ATTN:].astype(BF16),
        "ffn_norm_g": row(p["ffn_norm_g"][i]),
    }


def _split_hi_lo(w):
    hi = w.astype(BF16)
    lo = (w - hi.astype(F32)).astype(BF16)
    return hi, lo


def _tile_sizes(segs):
    s_min = min(s for _, _, s in segs)
    return {
        "tm": min(512, s_min),
        "tc": min(512, s_min),
        "rows": 64,
        "tq": min(512, s_min),
        "tk": min(512, s_min),
        "nb": 4,
        "tf": D_FF // 2,
        "tme": min(1024, s_min),
    }


def kernel(x_prompt, x_sample, attn_norm_g, w_in, q_norm_g, w_uq, kv_norm_g, w_ukv, conv_dw_w,
           conv_dw_b, conv_ln_g, conv_ln_b, attn_out_norm_g, conv_out_norm_g, w_out, ffn_norm_g,
           dense_w_gate, dense_w_up, dense_w_down, moe_w_router, moe_b_router, moe_w_gate,
           moe_w_up, moe_w_down, final_norm_g):
    p = dict(attn_norm_g=attn_norm_g, w_in=w_in, q_norm_g=q_norm_g, w_uq=w_uq,
             kv_norm_g=kv_norm_g, w_ukv=w_ukv, conv_dw_w=conv_dw_w, conv_dw_b=conv_dw_b,
             conv_ln_g=conv_ln_g, conv_ln_b=conv_ln_b, attn_out_norm_g=attn_out_norm_g,
             conv_out_norm_g=conv_out_norm_g, w_out=w_out, ffn_norm_g=ffn_norm_g)
    bp, sp, _ = x_prompt.shape
    bs, ss, _ = x_sample.shape
    n_p, n_s = bp * sp, bs * ss
    segs = ((0, bp, sp), (n_p, bs, ss))
    assert n_p % ss == 0 and n_p % sp == 0
    ts = _tile_sizes(segs)
    tabs = _rope_tables(max(sp, ss))
    x = jnp.concatenate([x_prompt.reshape(n_p, D_MODEL), x_sample.reshape(n_s, D_MODEL)], axis=0)

    assert DEPTH % 2 == 0 and n_p % ts["tme"] == 0
    for i in range(DEPTH):
        lw = _layer_weights(i, p)
        hc, q_t, k_all, v_t = _pre_call(x, lw, tabs, segs, ts["tm"])
        o_t = None
        for off, n, s in segs:
            o_t = _attn_call(q_t, k_all, v_t, o_t, off, n, s, ts["tq"], ts["tk"],
                             math.gcd(ts["nb"], s // ts["tk"]))
        ocn = _conv_call(hc, lw, segs, ts["tc"], ts["rows"])
        mixer_args = _mixer_args(o_t, ocn, x, lw)
        j = i // 2
        if i % 2 == 0:
            x = _dense_ffn_call(mixer_args, lw["ffn_norm_g"], dense_w_gate[j].astype(BF16),
                                dense_w_up[j].astype(BF16), dense_w_down[j].astype(BF16),
                                ts["tm"], ts["tf"])
        else:
            wrt = jnp.pad(moe_w_router[j].T, ((0, LANES - N_EXPERTS), (0, 0)))
            wrt_hi, wrt_lo = _split_hi_lo(wrt)
            final = (final_norm_g.reshape(1, D_MODEL), n_p) if i == DEPTH - 1 else None
            x = _moe_call(mixer_args, lw["ffn_norm_g"], wrt_hi, wrt_lo,
                          moe_b_router[j].reshape(N_EXPERTS, 1),
                          moe_w_gate[j].astype(BF16), moe_w_up[j].astype(BF16),
                          moe_w_down[j].astype(BF16), ts["tme"], final)

    y_p, y_s = x
    return (y_p.reshape(bp, sp, D_MODEL), y_s.reshape(bs, ss, D_MODEL))
```

```python
import functools
import math

import jax
import jax.numpy as jnp
import numpy as np
from jax import lax
from jax.experimental import pallas as pl
from jax.experimental.pallas import tpu as pltpu

D_MODEL = 1024
DEPTH = 4
D_ATTN = 512
D_CONV = 512
N_HEADS = 8
QK_NOPE_DIM = 64
QK_ROPE_DIM = 32
V_HEAD_DIM = 64
QK_HEAD_DIM = QK_NOPE_DIM + QK_ROPE_DIM
Q_LORA_RANK = 256
KV_LORA_RANK = 128
ROPE_THETA = 10000.0
CONV_WIDTH = 31
CONV_PAD = (CONV_WIDTH - 1) // 2
D_FF = 2816
N_EXPERTS = 8
D_EXPERT = 1024
EPS = 1e-6
OFF_CQ = 2 * D_CONV
OFF_CKV = OFF_CQ + Q_LORA_RANK
OFF_KR = OFF_CKV + KV_LORA_RANK

LANES = 128
SUBLANES = 8
HEAD_PAD = LANES
HALO = 16
BF16_SUBLANES = 16
V_ROWS = V_HEAD_DIM + BF16_SUBLANES
VMEM_LIMIT_BYTES = 56 * 1024 * 1024

BF16 = jnp.bfloat16
F32 = jnp.float32
NEG_BIG = -1e30
Q_SCALE = (QK_HEAD_DIM ** -0.5) * math.log2(math.e)


def _cparams(*sem):
    return pltpu.CompilerParams(dimension_semantics=sem, vmem_limit_bytes=VMEM_LIMIT_BYTES)


def _rms(x, g):
    return x * lax.rsqrt(jnp.mean(x * x, axis=-1, keepdims=True) + EPS) * g


def _dot(a, b):
    return jnp.dot(a, b, preferred_element_type=F32)


def _dot_nt(a, b):
    return lax.dot_general(a, b, (((1,), (1,)), ((), ())), preferred_element_type=F32)


def _pre_kernel(x_ref, g_ref, win_ref, qg_ref, wqt_ref, wqrt_ref, kvg_ref, wk_ref, wvt_ref,
                cosk_ref, sink_ref, cosq_ref, sinq_ref,
                hc_ref, q_ref, k_ref, vt_ref):
    x = x_ref[...]
    h = _rms(x, g_ref[...]).astype(BF16)
    z = _dot(h, win_ref[...])
    hc_ref[...] = z[:, :D_CONV] * jax.nn.sigmoid(z[:, D_CONV:2 * D_CONV])
    cqn = _rms(z[:, OFF_CQ:OFF_CKV], qg_ref[...]).astype(BF16)
    ckvn = _rms(z[:, OFF_CKV:OFF_KR], kvg_ref[...]).astype(BF16)
    kr = (z[:, OFF_KR:OFF_KR + LANES] * cosk_ref[...]
          + z[:, OFF_KR + LANES:OFF_KR + 2 * LANES] * sink_ref[...])
    kall = _dot(ckvn, wk_ref[...])
    for hd in range(N_HEADS):
        sl = slice(hd * HEAD_PAD, (hd + 1) * HEAD_PAD)
        k_ref[:, sl] = (kall[:, sl] + kr).astype(BF16)
    vt = _dot_nt(wvt_ref[...], ckvn)
    ones = jnp.ones((V_ROWS - V_HEAD_DIM, x.shape[0]), BF16)
    for hd in range(N_HEADS):
        vt_ref[hd, 0:V_HEAD_DIM, :] = vt[hd * V_HEAD_DIM:(hd + 1) * V_HEAD_DIM].astype(BF16)
        vt_ref[hd, V_HEAD_DIM:V_ROWS, :] = ones
    qt = _dot_nt(wqt_ref[...], cqn)
    qrt = _dot_nt(wqrt_ref[...], cqn)
    cq = cosq_ref[...]
    sq = sinq_ref[...]
    zeros = jnp.zeros((HEAD_PAD - QK_HEAD_DIM, x.shape[0]), BF16)
    for hd in range(N_HEADS):
        base = hd * HEAD_PAD
        q_ref[hd, 0:QK_NOPE_DIM, :] = (qt[base:base + QK_NOPE_DIM] * Q_SCALE).astype(BF16)
        rope = (qt[base + QK_NOPE_DIM:base + QK_HEAD_DIM] * cq
                + qrt[hd * QK_ROPE_DIM:(hd + 1) * QK_ROPE_DIM] * sq)
        q_ref[hd, QK_NOPE_DIM:QK_HEAD_DIM, :] = rope.astype(BF16)
        q_ref[hd, QK_HEAD_DIM:HEAD_PAD, :] = zeros


def _pre_call(x, lw, tabs, segs, tm):
    T = x.shape[0]
    n_tiles = T // tm
    pos_map = _pos_block_map(segs, tm)
    full = lambda shape: pl.BlockSpec(shape, lambda i: (0,) * len(shape))
    in_cols = lw["w_in"].shape[1]
    return pl.pallas_call(
        _pre_kernel,
        grid=(n_tiles,),
        in_specs=[
            pl.BlockSpec((tm, D_MODEL), lambda i: (i, 0)),
            full((1, D_MODEL)),
            full((D_MODEL, in_cols)),
            full((1, Q_LORA_RANK)),
            full((N_HEADS * HEAD_PAD, Q_LORA_RANK)),
            full((N_HEADS * QK_ROPE_DIM, Q_LORA_RANK)),
            full((1, KV_LORA_RANK)),
            full((KV_LORA_RANK, N_HEADS * HEAD_PAD)),
            full((N_HEADS * V_HEAD_DIM, KV_LORA_RANK)),
            pl.BlockSpec((tm, LANES), lambda i: (pos_map(i), 0)),
            pl.BlockSpec((tm, LANES), lambda i: (pos_map(i), 0)),
            pl.BlockSpec((QK_ROPE_DIM, tm), lambda i: (0, pos_map(i))),
            pl.BlockSpec((QK_ROPE_DIM, tm), lambda i: (0, pos_map(i))),
        ],
        out_specs=[
            pl.BlockSpec((tm, D_CONV), lambda i: (i, 0)),
            pl.BlockSpec((N_HEADS, HEAD_PAD, tm), lambda i: (0, 0, i)),
            pl.BlockSpec((tm, N_HEADS * HEAD_PAD), lambda i: (i, 0)),
            pl.BlockSpec((N_HEADS, V_ROWS, tm), lambda i: (0, 0, i)),
        ],
        out_shape=[
            jax.ShapeDtypeStruct((T, D_CONV), F32),
            jax.ShapeDtypeStruct((N_HEADS, HEAD_PAD, T), BF16),
            jax.ShapeDtypeStruct((T, N_HEADS * HEAD_PAD), BF16),
            jax.ShapeDtypeStruct((N_HEADS, V_ROWS, T), BF16),
        ],
        compiler_params=_cparams("parallel"),
        name="layer_front",
    )(x, lw["attn_norm_g"], lw["w_in"], lw["q_norm_g"], lw["wq_t"], lw["wqrot_t"],
      lw["kv_norm_g"], lw["wk"], lw["wv_t"], tabs["cosk"], tabs["sink"], tabs["cosq_t"],
      tabs["sinq_t"])


def _pos_block_map(segs, tile):
    def pos_map(i):
        out = None
        for off, n_seq, seq_len in reversed(segs):
            blk = (i - off // tile) % (seq_len // tile)
            out = blk if out is None else jnp.where(i < (off + n_seq * seq_len) // tile, blk, out)
        return out
    return pos_map


def _attn_kernel(q_ref, k_ref, vt_ref, o_prev_ref, o_ref, s_ref, cm_ref, *, tq, tk, nb):
    del o_prev_ref
    seq = k_ref.shape[0]
    n_kv = seq // tk
    n_body = n_kv // nb
    n_q = seq // tq
    look = nb - 1

    def q_tile(qi):
        return q_ref[0, :, pl.ds(pl.multiple_of(qi * tq, tq), tq)]

    def stage_a(q, j, slot):
        start = pl.multiple_of(j * tk, tk)
        s = _dot(k_ref[pl.ds(start, tk), :], q)
        s_ref[slot] = s
        cm_ref[slot] = jnp.max(s, axis=0, keepdims=True)

    def stage_bc(j, slot, m, acc):
        m_new = jnp.maximum(m, cm_ref[slot])
        alpha = jnp.exp2(m - m_new)
        p = jnp.exp2(s_ref[slot] - m_new).astype(BF16)
        start = pl.multiple_of(j * tk, tk)
        acc = alpha * acc + _dot(vt_ref[0, :, pl.ds(start, tk)], p)
        return m_new, acc

    def body(q_cur, q_next, j0, carry, last=False):
        m, acc = carry
        for i in range(nb):
            if last and i >= 1:
                stage_a(q_next, i - 1, (i + look) % nb)
            else:
                stage_a(q_cur, j0 + i + look, (i + look) % nb)
            m, acc = stage_bc(j0 + i, i, m, acc)
        return m, acc

    q_first = q_tile(0)
    for j in range(look):
        stage_a(q_first, j, j)

    def per_query_tile(qi, _):
        q_cur = q_tile(qi)
        q_next = q_tile(jnp.minimum(qi + 1, n_q - 1))
        carry = (jnp.full((1, tq), NEG_BIG, F32), jnp.zeros((V_ROWS, tq), F32))
        if n_body > 1:
            carry = lax.fori_loop(0, n_body - 1,
                                  lambda jb, c: body(q_cur, q_next, jb * nb, c), carry)
        _, acc = body(q_cur, q_next, (n_body - 1) * nb, carry, last=True)
        out = acc[:V_HEAD_DIM] / acc[V_HEAD_DIM:V_HEAD_DIM + 1]
        o_ref[:, pl.ds(pl.multiple_of(qi * tq, tq), tq)] = out.astype(o_ref.dtype)
        return 0

    lax.fori_loop(0, n_q, per_query_tile, 0)


def _attn_call(q_t, k_all, v_t, o_prev, tok_off, n_seq, seq_len, tq, tk, nb):
    assert (seq_len // tk) % nb == 0 and seq_len % tq == 0
    seq0 = tok_off // seq_len
    return pl.pallas_call(
        functools.partial(_attn_kernel, tq=tq, tk=tk, nb=nb),
        grid=(n_seq, N_HEADS),
        in_specs=[
            pl.BlockSpec((1, HEAD_PAD, seq_len), lambda b, h: (h, 0, seq0 + b)),
            pl.BlockSpec((seq_len, HEAD_PAD), lambda b, h: (seq0 + b, h)),
            pl.BlockSpec((1, V_ROWS, seq_len), lambda b, h: (h, 0, seq0 + b)),
            pl.BlockSpec(memory_space=pl.ANY),
        ],
        out_specs=pl.BlockSpec((V_HEAD_DIM, seq_len), lambda b, h: (h, seq0 + b)),
        out_shape=jax.ShapeDtypeStruct(o_prev.shape, o_prev.dtype),
        scratch_shapes=[pltpu.VMEM((nb, tk, tq), F32), pltpu.VMEM((nb, 1, tq), F32)],
        input_output_aliases={3: 0},
        compiler_params=_cparams("parallel", "parallel"),
        name="mla_attention",
    )(q_t, k_all, v_t, o_prev)


def _conv_kernel(cur_ref, prev_ref, next_ref, w_ref, b_ref, lng_ref, lnb_ref, og_ref, o_ref,
                 ext_ref, ph_ref, *, seg_tiles, rows):
    i = pl.program_id(0)
    tc = cur_ref.shape[0]
    is_start = _tile_is_boundary(i, seg_tiles, 0)
    is_end = _tile_is_boundary(i, seg_tiles, 1)
    ext_ref[0:HALO, :] = jnp.where(is_start, 0.0, prev_ref[...])
    ext_ref[HALO:HALO + tc, :] = cur_ref[...]
    ext_ref[HALO + tc:2 * HALO + tc, :] = jnp.where(is_end, 0.0, next_ref[...])
    span = ph_ref.shape[1]
    for b in range(SUBLANES):
        first = HALO - CONV_PAD + b
        ph_ref[b] = ext_ref[first:first + span, :]
    bias = b_ref[...]
    lng = lng_ref[...]
    lnb = lnb_ref[...]
    og = og_ref[...]
    for c in range(tc // rows):
        acc = jnp.broadcast_to(bias, (rows, D_CONV))
        for k in range(CONV_WIDTH):
            a, b = divmod(k, SUBLANES)
            first = c * rows + SUBLANES * a
            acc = acc + w_ref[k:k + 1, :] * ph_ref[b, first:first + rows, :]
        mu = jnp.mean(acc, axis=-1, keepdims=True)
        d = acc - mu
        var = jnp.mean(d * d, axis=-1, keepdims=True)
        y = d * lax.rsqrt(var + EPS) * lng + lnb
        y = y * jax.nn.sigmoid(y)
        o_ref[c * rows:(c + 1) * rows, :] = _rms(y, og).astype(o_ref.dtype)


def _tile_is_boundary(i, seg_tiles, end):
    out = None
    for first, n, per in reversed(seg_tiles):
        rel = (i - first + end) % per == 0
        out = rel if out is None else jnp.where(i < first + n, rel, out)
    return out


def _conv_call(hc, lw, segs, tc, rows):
    T = hc.shape[0]
    n_tiles = T // tc
    hb = tc // HALO
    last_halo = T // HALO - 1
    seg_tiles = tuple((off // tc, n * s // tc, s // tc) for off, n, s in segs)
    full = lambda shape: pl.BlockSpec(shape, lambda i: (0,) * len(shape))
    return pl.pallas_call(
        functools.partial(_conv_kernel, seg_tiles=seg_tiles, rows=rows),
        grid=(n_tiles,),
        in_specs=[
            pl.BlockSpec((tc, D_CONV), lambda i: (i, 0)),
            pl.BlockSpec((HALO, D_CONV), lambda i: (jnp.maximum(i * hb - 1, 0), 0)),
            pl.BlockSpec((HALO, D_CONV), lambda i: (jnp.minimum((i + 1) * hb, last_halo), 0)),
            full((CONV_WIDTH, D_CONV)),
            full((1, D_CONV)), full((1, D_CONV)), full((1, D_CONV)), full((1, D_CONV)),
        ],
        out_specs=pl.BlockSpec((tc, D_CONV), lambda i: (i, 0)),
        out_shape=jax.ShapeDtypeStruct((T, D_CONV), BF16),
        scratch_shapes=[pltpu.VMEM((tc + 2 * HALO, D_CONV), F32),
                        pltpu.VMEM((SUBLANES, tc + SUBLANES * ((CONV_WIDTH - 1) // SUBLANES),
                                    D_CONV), F32)],
        compiler_params=_cparams("parallel"),
        name="conv_branch",
    )(hc, hc, hc, lw["conv_dw_w"], lw["conv_dw_b"], lw["conv_ln_g"], lw["conv_ln_b"],
      lw["conv_out_norm_g"])


def _mixer_out(ot_ref, oc_ref, x_ref, ga_ref, wa_ref, wc_ref):
    ot = ot_ref[...].astype(F32)
    r = lax.rsqrt(jnp.mean(ot * ot, axis=0, keepdims=True) + EPS)
    oa = (jnp.transpose(ot * r) * ga_ref[...]).astype(BF16)
    return x_ref[...] + _dot(oa, wa_ref[...]) + _dot(oc_ref[...], wc_ref[...])


def _mixer_specs(tm):
    full2 = lambda shape: pl.BlockSpec(shape, lambda i, j: (0,) * len(shape))
    return [
        pl.BlockSpec((D_ATTN, tm), lambda i, j: (0, i)),
        pl.BlockSpec((tm, D_CONV), lambda i, j: (i, 0)),
        pl.BlockSpec((tm, D_MODEL), lambda i, j: (i, 0)),
        full2((1, D_ATTN)),
        full2((D_ATTN, D_MODEL)),
        full2((D_CONV, D_MODEL)),
    ]


def _mixer_args(o_t, ocn, x, lw):
    return (o_t, ocn, x, lw["attn_out_norm_g"], lw["w_out_a"], lw["w_out_c"])


def _dense_ffn_kernel(ot_ref, oc_ref, x_ref, ga_ref, wa_ref, wc_ref, g_ref, wg_ref, wu_ref,
                      wd_ref, y_ref, h_ref, acc_ref):
    j = pl.program_id(1)

    @pl.when(j == 0)
    def _():
        x = _mixer_out(ot_ref, oc_ref, x_ref, ga_ref, wa_ref, wc_ref)
        h_ref[...] = _rms(x, g_ref[...]).astype(BF16)
        acc_ref[...] = x

    h = h_ref[...]
    gate = _dot(h, wg_ref[...])
    up = _dot(h, wu_ref[...])
    a = (gate * jax.nn.sigmoid(gate) * up).astype(BF16)
    acc_ref[...] += _dot(a, wd_ref[...])

    @pl.when(j == pl.num_programs(1) - 1)
    def _():
        y_ref[...] = acc_ref[...]


def _dense_ffn_call(mixer_args, g, wg, wu, wd, tm, tf):
    T = mixer_args[2].shape[0]
    return pl.pallas_call(
        _dense_ffn_kernel,
        grid=(T // tm, D_FF // tf),
        in_specs=_mixer_specs(tm) + [
            pl.BlockSpec((1, D_MODEL), lambda i, j: (0, 0)),
            pl.BlockSpec((D_MODEL, tf), lambda i, j: (0, j)),
            pl.BlockSpec((D_MODEL, tf), lambda i, j: (0, j)),
            pl.BlockSpec((tf, D_MODEL), lambda i, j: (j, 0)),
        ],
        out_specs=pl.BlockSpec((tm, D_MODEL), lambda i, j: (i, 0)),
        out_shape=jax.ShapeDtypeStruct((T, D_MODEL), F32),
        scratch_shapes=[pltpu.VMEM((tm, D_MODEL), BF16), pltpu.VMEM((tm, D_MODEL), F32)],
        compiler_params=_cparams("parallel", "arbitrary"),
        name="dense_ffn",
    )(*mixer_args, g, wg, wu, wd)


MOE_SUB = 512
MOE_CHUNK = 9 * BF16_SUBLANES


def _router_top2_t(h_f32, wrt_hi, wrt_lo, br_col):
    h_hi = h_f32.astype(BF16)
    h_lo = (h_f32 - h_hi.astype(F32)).astype(BF16)
    logits = (_dot_nt(wrt_hi, h_hi) + _dot_nt(wrt_hi, h_lo) + _dot_nt(wrt_lo, h_hi))[:N_EXPERTS]
    logits = logits + br_col
    mx = jnp.max(logits, axis=0, keepdims=True)
    ex = jnp.exp(logits - mx)
    probs = ex / jnp.sum(ex, axis=0, keepdims=True)
    row = lax.broadcasted_iota(jnp.int32, probs.shape, 0).astype(F32)
    none = float(N_EXPERTS)
    p1 = jnp.max(probs, axis=0, keepdims=True)
    i1 = jnp.min(jnp.where(probs == p1, row, none), axis=0, keepdims=True)
    sel1 = row == i1
    rest = jnp.where(sel1, -1.0, probs)
    p2 = jnp.max(rest, axis=0, keepdims=True)
    i2 = jnp.min(jnp.where(rest == p2, row, none), axis=0, keepdims=True)
    sel2 = row == i2
    denom = p1 + p2
    gates = jnp.where(sel1, p1 / denom, 0.0) + jnp.where(sel2, p2 / denom, 0.0)
    mask = jnp.where(sel1 | sel2, 1.0, 0.0)
    return gates, mask


def _moe_kernel(ot_ref, oc_ref, x_ref, ga_ref, wa_ref, wc_ref, g_ref, wrh_ref, wrl_ref, br_ref,
                wg_ref, wu_ref, wd_ref, gf_ref, *rest, n_first):
    outs, (h_ref, gates_ref, mask_ref, rank_ref, acc_ref, cnt_ref) = rest[:-6], rest[-6:]
    i = pl.program_id(0)
    e = pl.program_id(1)
    tm = x_ref.shape[0]
    ts = min(MOE_SUB, tm)

    @pl.when(e == 0)
    def _():
        x = _mixer_out(ot_ref, oc_ref, x_ref, ga_ref, wa_ref, wc_ref)
        hf = _rms(x, g_ref[...])
        h_ref[...] = hf.astype(BF16)
        gates, mask = _router_top2_t(hf, wrh_ref[...], wrl_ref[...], br_ref[...])
        gates_ref[...] = gates
        mask_ref[...] = mask
        mask16 = jnp.concatenate([mask, jnp.zeros_like(mask)], axis=0).astype(BF16)
        before = (lax.broadcasted_iota(jnp.int32, (ts, ts), 0)
                  < lax.broadcasted_iota(jnp.int32, (ts, ts), 1))
        before = jnp.where(before, 1.0, 0.0).astype(BF16)
        for sub in range(tm // ts):
            cols = slice(sub * ts, (sub + 1) * ts)
            rank_ref[:, cols] = _dot(mask16[:, cols], before)[:N_EXPERTS]
            for ex in range(N_EXPERTS):
                cnt_ref[ex * (tm // ts) + sub] = jnp.sum(mask[ex:ex + 1, cols]).astype(jnp.int32)
        acc_ref[...] = x

    slot = lax.broadcasted_iota(jnp.int32, (MOE_CHUNK, ts), 0).astype(F32)
    for sub in range(tm // ts):
        cols = slice(sub * ts, (sub + 1) * ts)
        n_rows = cnt_ref[e * (tm // ts) + sub]
        n_chunks = lax.div(n_rows + (MOE_CHUNK - 1), MOE_CHUNK)
        rank_e = rank_ref[pl.ds(e, 1), cols]
        routed_e = mask_ref[pl.ds(e, 1), cols] > 0.5
        gate_e = gates_ref[pl.ds(e, 1), cols]

        def chunk(c, _, cols=cols, rank_e=rank_e, routed_e=routed_e, gate_e=gate_e):
            first = (c * MOE_CHUNK).astype(F32)
            sel = (rank_e == slot + first) & routed_e
            sel_b = jnp.where(sel, 1.0, 0.0).astype(BF16)
            xs = _dot(sel_b, h_ref[cols, :]).astype(BF16)
            gate = _dot(xs, wg_ref[0])
            up = _dot(xs, wu_ref[0])
            a = (gate * jax.nn.sigmoid(gate) * up).astype(BF16)
            out = _dot(a, wd_ref[0])
            row_gate = jnp.sum(jnp.where(sel, gate_e, 0.0), axis=1, keepdims=True)
            scaled = (out * row_gate).astype(BF16)
            acc_ref[cols, :] += lax.dot_general(sel_b, scaled, (((0,), (0,)), ((), ())),
                                                preferred_element_type=F32)
            return 0

        lax.fori_loop(0, n_chunks, chunk, 0)

    @pl.when(e == pl.num_programs(1) - 1)
    def _():
        if n_first is None:
            outs[0][...] = acc_ref[...]
        else:
            y = _rms(acc_ref[...], gf_ref[...])

            @pl.when(i < n_first)
            def _():
                outs[0][...] = y

            @pl.when(i >= n_first)
            def _():
                outs[1][...] = y


def _moe_call(mixer_args, g, wrt_hi, wrt_lo, br_col, wg, wu, wd, tm, final=None):
    T = mixer_args[2].shape[0]
    full2 = lambda shape: pl.BlockSpec(shape, lambda i, e: (0,) * len(shape))
    if final is None:
        gf, n_first = g, None
        out_specs = pl.BlockSpec((tm, D_MODEL), lambda i, e: (i, 0))
        out_shape = jax.ShapeDtypeStruct((T, D_MODEL), F32)
    else:
        gf, n_tok_first = final
        n_first = n_tok_first // tm
        out_specs = [pl.BlockSpec((tm, D_MODEL), lambda i, e: (jnp.minimum(i, n_first - 1), 0)),
                     pl.BlockSpec((tm, D_MODEL), lambda i, e: (jnp.maximum(i - n_first, 0), 0))]
        out_shape = [jax.ShapeDtypeStruct((n_tok_first, D_MODEL), F32),
                     jax.ShapeDtypeStruct((T - n_tok_first, D_MODEL), F32)]
    return pl.pallas_call(
        functools.partial(_moe_kernel, n_first=n_first),
        grid=(T // tm, N_EXPERTS),
        in_specs=_mixer_specs(tm) + [
            full2((1, D_MODEL)),
            full2((LANES, D_MODEL)),
            full2((LANES, D_MODEL)),
            full2((N_EXPERTS, 1)),
            pl.BlockSpec((1, D_MODEL, D_EXPERT), lambda i, e: (e, 0, 0)),
            pl.BlockSpec((1, D_MODEL, D_EXPERT), lambda i, e: (e, 0, 0)),
            pl.BlockSpec((1, D_EXPERT, D_MODEL), lambda i, e: (e, 0, 0)),
            full2((1, D_MODEL)),
        ],
        out_specs=out_specs,
        out_shape=out_shape,
        scratch_shapes=[pltpu.VMEM((tm, D_MODEL), BF16),
                        pltpu.VMEM((N_EXPERTS, tm), F32), pltpu.VMEM((N_EXPERTS, tm), F32),
                        pltpu.VMEM((N_EXPERTS, tm), F32),
                        pltpu.VMEM((tm, D_MODEL), F32),
                        pltpu.SMEM((N_EXPERTS * (tm // min(MOE_SUB, tm)),), jnp.int32)],
        compiler_params=_cparams("arbitrary", "arbitrary"),
        name="moe_ffn",
    )(*mixer_args, g, wrt_hi, wrt_lo, br_col, wg, wu, wd, gf)


def _rope_tables(max_len):
    pos = jnp.arange(max_len, dtype=F32)
    freqs = ROPE_THETA ** (-(jnp.arange(0, QK_ROPE_DIM, 2, dtype=F32) / QK_ROPE_DIM))
    ang = pos[:, None] * freqs[None, :]
    ang = jnp.concatenate([ang, ang], axis=-1)
    cos, sin = jnp.cos(ang), jnp.sin(ang)
    pad = lambda t: jnp.pad(t, ((0, 0), (QK_NOPE_DIM, HEAD_PAD - QK_HEAD_DIM)))
    return {"cosk": pad(cos), "sink": pad(sin),
            "cosq_t": (cos * Q_SCALE).T, "sinq_t": (sin * Q_SCALE).T}


def _rot_cols(w):
    half = QK_ROPE_DIM // 2
    return jnp.concatenate([-w[..., half:], w[..., :half]], axis=-1)


def _layer_weights(i, p):
    row = lambda v: v.reshape(1, -1)
    w_in = p["w_in"][i]
    w_kr = w_in[:, OFF_KR:]
    place = lambda w: jnp.pad(w, ((0, 0), (QK_NOPE_DIM, HEAD_PAD - QK_HEAD_DIM)))
    w_in_ext = jnp.concatenate([w_in[:, :OFF_KR], place(w_kr), place(_rot_cols(w_kr))], axis=1)
    w_uq = p["w_uq"][i].reshape(Q_LORA_RANK, N_HEADS, QK_HEAD_DIM)
    wq_pad = jnp.pad(w_uq, ((0, 0), (0, 0), (0, HEAD_PAD - QK_HEAD_DIM)))
    wq_t = wq_pad.reshape(Q_LORA_RANK, N_HEADS * HEAD_PAD).T
    wqrot_t = _rot_cols(w_uq[..., QK_NOPE_DIM:]).reshape(Q_LORA_RANK, N_HEADS * QK_ROPE_DIM).T
    w_ukv = p["w_ukv"][i].reshape(KV_LORA_RANK, N_HEADS, QK_NOPE_DIM + V_HEAD_DIM)
    wk = jnp.pad(w_ukv[..., :QK_NOPE_DIM], ((0, 0), (0, 0), (0, HEAD_PAD - QK_NOPE_DIM)))
    wk = wk.reshape(KV_LORA_RANK, N_HEADS * HEAD_PAD)
    wv_t = w_ukv[..., QK_NOPE_DIM:].reshape(KV_LORA_RANK, N_HEADS * V_HEAD_DIM).T
    w_out = p["w_out"][i]
    return {
        "attn_norm_g": row(p["attn_norm_g"][i]),
        "w_in": w_in_ext.astype(BF16),
        "q_norm_g": row(p["q_norm_g"][i]),
        "wq_t": wq_t.astype(BF16),
        "wqrot_t": wqrot_t.astype(BF16),
        "kv_norm_g": row(p["kv_norm_g"][i]),
        "wk": wk.astype(BF16),
        "wv_t": wv_t.astype(BF16),
        "conv_dw_w": p["conv_dw_w"][i].reshape(CONV_WIDTH, D_CONV),
        "conv_dw_b": row(p["conv_dw_b"][i]),
        "conv_ln_g": row(p["conv_ln_g"][i]),
        "conv_ln_b": row(p["conv_ln_b"][i]),
        "conv_out_norm_g": row(p["conv_out_norm_g"][i]),
        "attn_out_norm_g": row(p["attn_out_norm_g"][i]),
        "w_out_a": w_out[:D_ATTN].astype(BF16),
        "w_out_c": w_out[D_ATTN:].astype(BF16),
        "ffn_norm_g": row(p["ffn_norm_g"][i]),
    }


def _split_hi_lo(w):
    hi = w.astype(BF16)
    lo = (w - hi.astype(F32)).astype(BF16)
    return hi, lo


def _tile_sizes(segs):
    s_min = min(s for _, _, s in segs)
    return {
        "tm": min(512, s_min),
        "tc": min(512, s_min),
        "rows": 64,
        "tq": min(512, s_min),
        "tk": min(512, s_min),
        "nb": 8,
        "tf": D_FF // 2,
        "tme": min(1024, s_min),
    }


def kernel(x_prompt, x_sample, attn_norm_g, w_in, q_norm_g, w_uq, kv_norm_g, w_ukv, conv_dw_w,
           conv_dw_b, conv_ln_g, conv_ln_b, attn_out_norm_g, conv_out_norm_g, w_out, ffn_norm_g,
           dense_w_gate, dense_w_up, dense_w_down, moe_w_router, moe_b_router, moe_w_gate,
           moe_w_up, moe_w_down, final_norm_g):
    p = dict(attn_norm_g=attn_norm_g, w_in=w_in, q_norm_g=q_norm_g, w_uq=w_uq,
             kv_norm_g=kv_norm_g, w_ukv=w_ukv, conv_dw_w=conv_dw_w, conv_dw_b=conv_dw_b,
             conv_ln_g=conv_ln_g, conv_ln_b=conv_ln_b, attn_out_norm_g=attn_out_norm_g,
             conv_out_norm_g=conv_out_norm_g, w_out=w_out, ffn_norm_g=ffn_norm_g)
    bp, sp, _ = x_prompt.shape
    bs, ss, _ = x_sample.shape
    n_p, n_s = bp * sp, bs * ss
    segs = ((0, bp, sp), (n_p, bs, ss))
    assert n_p % ss == 0 and n_p % sp == 0
    ts = _tile_sizes(segs)
    tabs = _rope_tables(max(sp, ss))
    x = jnp.concatenate([x_prompt.reshape(n_p, D_MODEL), x_sample.reshape(n_s, D_MODEL)], axis=0)

    assert DEPTH % 2 == 0 and n_p % ts["tme"] == 0
    for i in range(DEPTH):
        lw = _layer_weights(i, p)
        hc, q_t, k_all, v_t = _pre_call(x, lw, tabs, segs, ts["tm"])
        o_t = jnp.zeros((D_ATTN, n_p + n_s), BF16)
        for off, n, s in segs:
            o_t = _attn_call(q_t, k_all, v_t, o_t, off, n, s, ts["tq"], ts["tk"],
                             math.gcd(ts["nb"], s // ts["tk"]))
        ocn = _conv_call(hc, lw, segs, ts["tc"], ts["rows"])
        mixer_args = _mixer_args(o_t, ocn, x, lw)
        j = i // 2
        if i % 2 == 0:
            x = _dense_ffn_call(mixer_args, lw["ffn_norm_g"], dense_w_gate[j].astype(BF16),
                                dense_w_up[j].astype(BF16), dense_w_down[j].astype(BF16),
                                ts["tm"], ts["tf"])
        else:
            wrt = jnp.pad(moe_w_router[j].T, ((0, LANES - N_EXPERTS), (0, 0)))
            wrt_hi, wrt_lo = _split_hi_lo(wrt)
            final = (final_norm_g.reshape(1, D_MODEL), n_p) if i == DEPTH - 1 else None
            x = _moe_call(mixer_args, lw["ffn_norm_g"], wrt_hi, wrt_lo,
                          moe_b_router[j].reshape(N_EXPERTS, 1),
                          moe_w_gate[j].astype(BF16), moe_w_up[j].astype(BF16),
                          moe_w_down[j].astype(BF16), ts["tme"], final)

    y_p, y_s = x
    return (y_p.reshape(bp, sp, D_MODEL), y_s.reshape(bs, ss, D_MODEL))
```

```python
import functools
import math

import jax
import jax.numpy as jnp
import numpy as np
from jax import lax
from jax.experimental import pallas as pl
from jax.experimental.pallas import tpu as pltpu

D_MODEL = 1024
DEPTH = 4
D_ATTN = 512
D_CONV = 512
N_HEADS = 8
QK_NOPE_DIM = 64
QK_ROPE_DIM = 32
V_HEAD_DIM = 64
QK_HEAD_DIM = QK_NOPE_DIM + QK_ROPE_DIM
Q_LORA_RANK = 256
KV_LORA_RANK = 128
ROPE_THETA = 10000.0
CONV_WIDTH = 31
CONV_PAD = (CONV_WIDTH - 1) // 2
D_FF = 2816
N_EXPERTS = 8
D_EXPERT = 1024
EPS = 1e-6
OFF_CQ = 2 * D_CONV
OFF_CKV = OFF_CQ + Q_LORA_RANK
OFF_KR = OFF_CKV + KV_LORA_RANK

LANES = 128
SUBLANES = 8
HEAD_PAD = LANES
HALO = 16
BF16_SUBLANES = 16
V_ROWS = V_HEAD_DIM + BF16_SUBLANES
VMEM_LIMIT_BYTES = 56 * 1024 * 1024

BF16 = jnp.bfloat16
F32 = jnp.float32
NEG_BIG = -1e30
Q_SCALE = (QK_HEAD_DIM ** -0.5) * math.log2(math.e)


def _cparams(*sem):
    return pltpu.CompilerParams(dimension_semantics=sem, vmem_limit_bytes=VMEM_LIMIT_BYTES)


def _rms(x, g):
    return x * lax.rsqrt(jnp.mean(x * x, axis=-1, keepdims=True) + EPS) * g


def _dot(a, b):
    return jnp.dot(a, b, preferred_element_type=F32)


def _dot_nt(a, b):
    return lax.dot_general(a, b, (((1,), (1,)), ((), ())), preferred_element_type=F32)


def _pre_kernel(x_ref, g_ref, win_ref, qg_ref, wqt_ref, wqrt_ref, kvg_ref, wk_ref, wvt_ref,
                cosk_ref, sink_ref, cosq_ref, sinq_ref,
                hc_ref, q_ref, k_ref, vt_ref):
    x = x_ref[...]
    h = _rms(x, g_ref[...]).astype(BF16)
    z = _dot(h, win_ref[...])
    hc_ref[...] = z[:, :D_CONV] * jax.nn.sigmoid(z[:, D_CONV:2 * D_CONV])
    cqn = _rms(z[:, OFF_CQ:OFF_CKV], qg_ref[...]).astype(BF16)
    ckvn = _rms(z[:, OFF_CKV:OFF_KR], kvg_ref[...]).astype(BF16)
    kr = (z[:, OFF_KR:OFF_KR + LANES] * cosk_ref[...]
          + z[:, OFF_KR + LANES:OFF_KR + 2 * LANES] * sink_ref[...])
    kall = _dot(ckvn, wk_ref[...])
    for hd in range(N_HEADS):
        sl = slice(hd * HEAD_PAD, (hd + 1) * HEAD_PAD)
        k_ref[:, sl] = (kall[:, sl] + kr).astype(BF16)
    vt = _dot_nt(wvt_ref[...], ckvn)
    ones = jnp.ones((V_ROWS - V_HEAD_DIM, x.shape[0]), BF16)
    for hd in range(N_HEADS):
        vt_ref[hd, 0:V_HEAD_DIM, :] = vt[hd * V_HEAD_DIM:(hd + 1) * V_HEAD_DIM].astype(BF16)
        vt_ref[hd, V_HEAD_DIM:V_ROWS, :] = ones
    qt = _dot_nt(wqt_ref[...], cqn)
    qrt = _dot_nt(wqrt_ref[...], cqn)
    cq = cosq_ref[...]
    sq = sinq_ref[...]
    zeros = jnp.zeros((HEAD_PAD - QK_HEAD_DIM, x.shape[0]), BF16)
    for hd in range(N_HEADS):
        base = hd * HEAD_PAD
        q_ref[hd, 0:QK_NOPE_DIM, :] = (qt[base:base + QK_NOPE_DIM] * Q_SCALE).astype(BF16)
        rope = (qt[base + QK_NOPE_DIM:base + QK_HEAD_DIM] * cq
                + qrt[hd * QK_ROPE_DIM:(hd + 1) * QK_ROPE_DIM] * sq)
        q_ref[hd, QK_NOPE_DIM:QK_HEAD_DIM, :] = rope.astype(BF16)
        q_ref[hd, QK_HEAD_DIM:HEAD_PAD, :] = zeros


def _pre_call(x, lw, tabs, segs, tm):
    T = x.shape[0]
    n_tiles = T // tm
    pos_map = _pos_block_map(segs, tm)
    full = lambda shape: pl.BlockSpec(shape, lambda i: (0,) * len(shape))
    in_cols = lw["w_in"].shape[1]
    return pl.pallas_call(
        _pre_kernel,
        grid=(n_tiles,),
        in_specs=[
            pl.BlockSpec((tm, D_MODEL), lambda i: (i, 0)),
            full((1, D_MODEL)),
            full((D_MODEL, in_cols)),
            full((1, Q_LORA_RANK)),
            full((N_HEADS * HEAD_PAD, Q_LORA_RANK)),
            full((N_HEADS * QK_ROPE_DIM, Q_LORA_RANK)),
            full((1, KV_LORA_RANK)),
            full((KV_LORA_RANK, N_HEADS * HEAD_PAD)),
            full((N_HEADS * V_HEAD_DIM, KV_LORA_RANK)),
            pl.BlockSpec((tm, LANES), lambda i: (pos_map(i), 0)),
            pl.BlockSpec((tm, LANES), lambda i: (pos_map(i), 0)),
            pl.BlockSpec((QK_ROPE_DIM, tm), lambda i: (0, pos_map(i))),
            pl.BlockSpec((QK_ROPE_DIM, tm), lambda i: (0, pos_map(i))),
        ],
        out_specs=[
            pl.BlockSpec((tm, D_CONV), lambda i: (i, 0)),
            pl.BlockSpec((N_HEADS, HEAD_PAD, tm), lambda i: (0, 0, i)),
            pl.BlockSpec((tm, N_HEADS * HEAD_PAD), lambda i: (i, 0)),
            pl.BlockSpec((N_HEADS, V_ROWS, tm), lambda i: (0, 0, i)),
        ],
        out_shape=[
            jax.ShapeDtypeStruct((T, D_CONV), F32),
            jax.ShapeDtypeStruct((N_HEADS, HEAD_PAD, T), BF16),
            jax.ShapeDtypeStruct((T, N_HEADS * HEAD_PAD), BF16),
            jax.ShapeDtypeStruct((N_HEADS, V_ROWS, T), BF16),
        ],
        compiler_params=_cparams("parallel"),
        name="layer_front",
    )(x, lw["attn_norm_g"], lw["w_in"], lw["q_norm_g"], lw["wq_t"], lw["wqrot_t"],
      lw["kv_norm_g"], lw["wk"], lw["wv_t"], tabs["cosk"], tabs["sink"], tabs["cosq_t"],
      tabs["sinq_t"])


def _pos_block_map(segs, tile):
    def pos_map(i):
        out = None
        for off, n_seq, seq_len in reversed(segs):
            blk = (i - off // tile) % (seq_len // tile)
            out = blk if out is None else jnp.where(i < (off + n_seq * seq_len) // tile, blk, out)
        return out
    return pos_map


def _attn_kernel(q_ref, k_ref, vt_ref, o_prev_ref, o_ref, s_ref, cm_ref, *, tq, tk, nb):
    del o_prev_ref
    seq = k_ref.shape[0]
    n_kv = seq // tk
    n_body = n_kv // nb
    n_q = seq // tq
    look = nb - 1

    def q_tile(qi):
        return q_ref[0, :, pl.ds(pl.multiple_of(qi * tq, tq), tq)]

    def stage_a(q, j, slot):
        start = pl.multiple_of(j * tk, tk)
        s = _dot(k_ref[pl.ds(start, tk), :], q)
        s_ref[slot] = s
        cm_ref[slot] = jnp.max(s, axis=0, keepdims=True)

    def stage_bc(j, slot, m, acc):
        m_new = jnp.maximum(m, cm_ref[slot])
        alpha = jnp.exp2(m - m_new)
        p = jnp.exp2(s_ref[slot] - m_new).astype(BF16)
        start = pl.multiple_of(j * tk, tk)
        acc = alpha * acc + _dot(vt_ref[0, :, pl.ds(start, tk)], p)
        return m_new, acc

    def body(q_cur, q_next, j0, carry, last=False):
        m, acc = carry
        for i in range(nb):
            if last and i >= 1:
                stage_a(q_next, i - 1, (i + look) % nb)
            else:
                stage_a(q_cur, j0 + i + look, (i + look) % nb)
            m, acc = stage_bc(j0 + i, i, m, acc)
        return m, acc

    q_first = q_tile(0)
    for j in range(look):
        stage_a(q_first, j, j)

    def per_query_tile(qi, _):
        q_cur = q_tile(qi)
        q_next = q_tile(jnp.minimum(qi + 1, n_q - 1))
        carry = (jnp.full((1, tq), NEG_BIG, F32), jnp.zeros((V_ROWS, tq), F32))
        if n_body > 1:
            carry = lax.fori_loop(0, n_body - 1,
                                  lambda jb, c: body(q_cur, q_next, jb * nb, c), carry)
        _, acc = body(q_cur, q_next, (n_body - 1) * nb, carry, last=True)
        out = acc[:V_HEAD_DIM] / acc[V_HEAD_DIM:V_HEAD_DIM + 1]
        o_ref[:, pl.ds(pl.multiple_of(qi * tq, tq), tq)] = out.astype(o_ref.dtype)
        return 0

    lax.fori_loop(0, n_q, per_query_tile, 0)


def _attn_call(q_t, k_all, v_t, o_prev, tok_off, n_seq, seq_len, tq, tk, nb):
    assert (seq_len // tk) % nb == 0 and seq_len % tq == 0
    seq0 = tok_off // seq_len
    return pl.pallas_call(
        functools.partial(_attn_kernel, tq=tq, tk=tk, nb=nb),
        grid=(n_seq, N_HEADS),
        in_specs=[
            pl.BlockSpec((1, HEAD_PAD, seq_len), lambda b, h: (h, 0, seq0 + b)),
            pl.BlockSpec((seq_len, HEAD_PAD), lambda b, h: (seq0 + b, h)),
            pl.BlockSpec((1, V_ROWS, seq_len), lambda b, h: (h, 0, seq0 + b)),
            pl.BlockSpec(memory_space=pl.ANY),
        ],
        out_specs=pl.BlockSpec((V_HEAD_DIM, seq_len), lambda b, h: (h, seq0 + b)),
        out_shape=jax.ShapeDtypeStruct(o_prev.shape, o_prev.dtype),
        scratch_shapes=[pltpu.VMEM((nb, tk, tq), F32), pltpu.VMEM((nb, 1, tq), F32)],
        input_output_aliases={3: 0},
        compiler_params=_cparams("parallel", "parallel"),
        name="mla_attention",
    )(q_t, k_all, v_t, o_prev)


def _conv_kernel(cur_ref, prev_ref, next_ref, w_ref, b_ref, lng_ref, lnb_ref, og_ref, o_ref,
                 ext_ref, ph_ref, *, seg_tiles, rows):
    i = pl.program_id(0)
    tc = cur_ref.shape[0]
    is_start = _tile_is_boundary(i, seg_tiles, 0)
    is_end = _tile_is_boundary(i, seg_tiles, 1)
    ext_ref[0:HALO, :] = jnp.where(is_start, 0.0, prev_ref[...])
    ext_ref[HALO:HALO + tc, :] = cur_ref[...]
    ext_ref[HALO + tc:2 * HALO + tc, :] = jnp.where(is_end, 0.0, next_ref[...])
    span = ph_ref.shape[1]
    for b in range(SUBLANES):
        first = HALO - CONV_PAD + b
        ph_ref[b] = ext_ref[first:first + span, :]
    bias = b_ref[...]
    lng = lng_ref[...]
    lnb = lnb_ref[...]
    og = og_ref[...]
    for c in range(tc // rows):
        acc = jnp.broadcast_to(bias, (rows, D_CONV))
        for k in range(CONV_WIDTH):
            a, b = divmod(k, SUBLANES)
            first = c * rows + SUBLANES * a
            acc = acc + w_ref[k:k + 1, :] * ph_ref[b, first:first + rows, :]
        mu = jnp.mean(acc, axis=-1, keepdims=True)
        d = acc - mu
        var = jnp.mean(d * d, axis=-1, keepdims=True)
        y = d * lax.rsqrt(var + EPS) * lng + lnb
        y = y * jax.nn.sigmoid(y)
        o_ref[c * rows:(c + 1) * rows, :] = _rms(y, og).astype(o_ref.dtype)


def _tile_is_boundary(i, seg_tiles, end):
    out = None
    for first, n, per in reversed(seg_tiles):
        rel = (i - first + end) % per == 0
        out = rel if out is None else jnp.where(i < first + n, rel, out)
    return out


def _conv_call(hc, lw, segs, tc, rows):
    T = hc.shape[0]
    n_tiles = T // tc
    hb = tc // HALO
    last_halo = T // HALO - 1
    seg_tiles = tuple((off // tc, n * s // tc, s // tc) for off, n, s in segs)
    full = lambda shape: pl.BlockSpec(shape, lambda i: (0,) * len(shape))
    return pl.pallas_call(
        functools.partial(_conv_kernel, seg_tiles=seg_tiles, rows=rows),
        grid=(n_tiles,),
        in_specs=[
            pl.BlockSpec((tc, D_CONV), lambda i: (i, 0)),
            pl.BlockSpec((HALO, D_CONV), lambda i: (jnp.maximum(i * hb - 1, 0), 0)),
            pl.BlockSpec((HALO, D_CONV), lambda i: (jnp.minimum((i + 1) * hb, last_halo), 0)),
            full((CONV_WIDTH, D_CONV)),
            full((1, D_CONV)), full((1, D_CONV)), full((1, D_CONV)), full((1, D_CONV)),
        ],
        out_specs=pl.BlockSpec((tc, D_CONV), lambda i: (i, 0)),
        out_shape=jax.ShapeDtypeStruct((T, D_CONV), BF16),
        scratch_shapes=[pltpu.VMEM((tc + 2 * HALO, D_CONV), F32),
                        pltpu.VMEM((SUBLANES, tc + SUBLANES * ((CONV_WIDTH - 1) // SUBLANES),
                                    D_CONV), F32)],
        compiler_params=_cparams("parallel"),
        name="conv_branch",
    )(hc, hc, hc, lw["conv_dw_w"], lw["conv_dw_b"], lw["conv_ln_g"], lw["conv_ln_b"],
      lw["conv_out_norm_g"])


def _mixer_out(ot_ref, oc_ref, x_ref, ga_ref, wa_ref, wc_ref):
    ot = ot_ref[...].astype(F32)
    r = lax.rsqrt(jnp.mean(ot * ot, axis=0, keepdims=True) + EPS)
    oa = (jnp.transpose(ot * r) * ga_ref[...]).astype(BF16)
    return x_ref[...] + _dot(oa, wa_ref[...]) + _dot(oc_ref[...], wc_ref[...])


def _mixer_specs(tm):
    full2 = lambda shape: pl.BlockSpec(shape, lambda i, j: (0,) * len(shape))
    return [
        pl.BlockSpec((D_ATTN, tm), lambda i, j: (0, i)),
        pl.BlockSpec((tm, D_CONV), lambda i, j: (i, 0)),
        pl.BlockSpec((tm, D_MODEL), lambda i, j: (i, 0)),
        full2((1, D_ATTN)),
        full2((D_ATTN, D_MODEL)),
        full2((D_CONV, D_MODEL)),
    ]


def _mixer_args(o_t, ocn, x, lw):
    return (o_t, ocn, x, lw["attn_out_norm_g"], lw["w_out_a"], lw["w_out_c"])


def _dense_ffn_kernel(ot_ref, oc_ref, x_ref, ga_ref, wa_ref, wc_ref, g_ref, wg_ref, wu_ref,
                      wd_ref, y_ref, h_ref, acc_ref):
    j = pl.program_id(1)

    @pl.when(j == 0)
    def _():
        x = _mixer_out(ot_ref, oc_ref, x_ref, ga_ref, wa_ref, wc_ref)
        h_ref[...] = _rms(x, g_ref[...]).astype(BF16)
        acc_ref[...] = x

    h = h_ref[...]
    gate = _dot(h, wg_ref[...])
    up = _dot(h, wu_ref[...])
    a = (gate * jax.nn.sigmoid(gate) * up).astype(BF16)
    acc_ref[...] += _dot(a, wd_ref[...])

    @pl.when(j == pl.num_programs(1) - 1)
    def _():
        y_ref[...] = acc_ref[...]


def _dense_ffn_call(mixer_args, g, wg, wu, wd, tm, tf):
    T = mixer_args[2].shape[0]
    return pl.pallas_call(
        _dense_ffn_kernel,
        grid=(T // tm, D_FF // tf),
        in_specs=_mixer_specs(tm) + [
            pl.BlockSpec((1, D_MODEL), lambda i, j: (0, 0)),
            pl.BlockSpec((D_MODEL, tf), lambda i, j: (0, j)),
            pl.BlockSpec((D_MODEL, tf), lambda i, j: (0, j)),
            pl.BlockSpec((tf, D_MODEL), lambda i, j: (j, 0)),
        ],
        out_specs=pl.BlockSpec((tm, D_MODEL), lambda i, j: (i, 0)),
        out_shape=jax.ShapeDtypeStruct((T, D_MODEL), F32),
        scratch_shapes=[pltpu.VMEM((tm, D_MODEL), BF16), pltpu.VMEM((tm, D_MODEL), F32)],
        compiler_params=_cparams("parallel", "arbitrary"),
        name="dense_ffn",
    )(*mixer_args, g, wg, wu, wd)


MOE_SUB = 512
MOE_CHUNK = 9 * BF16_SUBLANES


def _router_top2_t(h_f32, wrt_hi, wrt_lo, br_col):
    h_hi = h_f32.astype(BF16)
    h_lo = (h_f32 - h_hi.astype(F32)).astype(BF16)
    logits = (_dot_nt(wrt_hi, h_hi) + _dot_nt(wrt_hi, h_lo) + _dot_nt(wrt_lo, h_hi))[:N_EXPERTS]
    logits = logits + br_col
    mx = jnp.max(logits, axis=0, keepdims=True)
    ex = jnp.exp(logits - mx)
    probs = ex / jnp.sum(ex, axis=0, keepdims=True)
    row = lax.broadcasted_iota(jnp.int32, probs.shape, 0).astype(F32)
    none = float(N_EXPERTS)
    p1 = jnp.max(probs, axis=0, keepdims=True)
    i1 = jnp.min(jnp.where(probs == p1, row, none), axis=0, keepdims=True)
    sel1 = row == i1
    rest = jnp.where(sel1, -1.0, probs)
    p2 = jnp.max(rest, axis=0, keepdims=True)
    i2 = jnp.min(jnp.where(rest == p2, row, none), axis=0, keepdims=True)
    sel2 = row == i2
    denom = p1 + p2
    gates = jnp.where(sel1, p1 / denom, 0.0) + jnp.where(sel2, p2 / denom, 0.0)
    mask = jnp.where(sel1 | sel2, 1.0, 0.0)
    return gates, mask


def _moe_kernel(ot_ref, oc_ref, x_ref, ga_ref, wa_ref, wc_ref, g_ref, wrh_ref, wrl_ref, br_ref,
                wg_ref, wu_ref, wd_ref, gf_ref, *rest, n_first):
    outs, (h_ref, gates_ref, mask_ref, rank_ref, acc_ref, cnt_ref) = rest[:-6], rest[-6:]
    i = pl.program_id(0)
    e = pl.program_id(1)
    tm = x_ref.shape[0]
    ts = min(MOE_SUB, tm)

    @pl.when(e == 0)
    def _():
        x = _mixer_out(ot_ref, oc_ref, x_ref, ga_ref, wa_ref, wc_ref)
        hf = _rms(x, g_ref[...])
        h_ref[...] = hf.astype(BF16)
        gates, mask = _router_top2_t(hf, wrh_ref[...], wrl_ref[...], br_ref[...])
        gates_ref[...] = gates
        mask_ref[...] = mask
        mask16 = jnp.concatenate([mask, jnp.zeros_like(mask)], axis=0).astype(BF16)
        before = (lax.broadcasted_iota(jnp.int32, (ts, ts), 0)
                  < lax.broadcasted_iota(jnp.int32, (ts, ts), 1))
        before = jnp.where(before, 1.0, 0.0).astype(BF16)
        for sub in range(tm // ts):
            cols = slice(sub * ts, (sub + 1) * ts)
            rank_ref[:, cols] = _dot(mask16[:, cols], before)[:N_EXPERTS]
            for ex in range(N_EXPERTS):
                cnt_ref[ex * (tm // ts) + sub] = jnp.sum(mask[ex:ex + 1, cols]).astype(jnp.int32)
        acc_ref[...] = x

    slot = lax.broadcasted_iota(jnp.int32, (MOE_CHUNK, ts), 0).astype(F32)
    for sub in range(tm // ts):
        cols = slice(sub * ts, (sub + 1) * ts)
        n_rows = cnt_ref[e * (tm // ts) + sub]
        n_chunks = lax.div(n_rows + (MOE_CHUNK - 1), MOE_CHUNK)
        rank_e = rank_ref[pl.ds(e, 1), cols]
        routed_e = mask_ref[pl.ds(e, 1), cols] > 0.5
        gate_e = gates_ref[pl.ds(e, 1), cols]

        def chunk(c, _, cols=cols, rank_e=rank_e, routed_e=routed_e, gate_e=gate_e):
            first = (c * MOE_CHUNK).astype(F32)
            sel = (rank_e == slot + first) & routed_e
            sel_b = jnp.where(sel, 1.0, 0.0).astype(BF16)
            xs = _dot(sel_b, h_ref[cols, :]).astype(BF16)
            gate = _dot(xs, wg_ref[0])
            up = _dot(xs, wu_ref[0])
            a = (gate * jax.nn.sigmoid(gate) * up).astype(BF16)
            out = _dot(a, wd_ref[0])
            row_gate = jnp.sum(jnp.where(sel, gate_e, 0.0), axis=1, keepdims=True)
            scaled = (out * row_gate).astype(BF16)
            acc_ref[cols, :] += lax.dot_general(sel_b, scaled, (((0,), (0,)), ((), ())),
                                                preferred_element_type=F32)
            return 0

        lax.fori_loop(0, n_chunks, chunk, 0)

    @pl.when(e == pl.num_programs(1) - 1)
    def _():
        if n_first is None:
            outs[0][...] = acc_ref[...]
        else:
            y = _rms(acc_ref[...], gf_ref[...])

            @pl.when(i < n_first)
            def _():
                outs[0][...] = y

            @pl.when(i >= n_first)
            def _():
                outs[1][...] = y


def _moe_call(mixer_args, g, wrt_hi, wrt_lo, br_col, wg, wu, wd, tm, final=None):
    T = mixer_args[2].shape[0]
    full2 = lambda shape: pl.BlockSpec(shape, lambda i, e: (0,) * len(shape))
    if final is None:
        gf, n_first = g, None
        out_specs = pl.BlockSpec((tm, D_MODEL), lambda i, e: (i, 0))
        out_shape = jax.ShapeDtypeStruct((T, D_MODEL), F32)
    else:
        gf, n_tok_first = final
        n_first = n_tok_first // tm
        out_specs = [pl.BlockSpec((tm, D_MODEL), lambda i, e: (jnp.minimum(i, n_first - 1), 0)),
                     pl.BlockSpec((tm, D_MODEL), lambda i, e: (jnp.maximum(i - n_first, 0), 0))]
        out_shape = [jax.ShapeDtypeStruct((n_tok_first, D_MODEL), F32),
                     jax.ShapeDtypeStruct((T - n_tok_first, D_MODEL), F32)]
    return pl.pallas_call(
        functools.partial(_moe_kernel, n_first=n_first),
        grid=(T // tm, N_EXPERTS),
        in_specs=_mixer_specs(tm) + [
            full2((1, D_MODEL)),
            full2((LANES, D_MODEL)),
            full2((LANES, D_MODEL)),
            full2((N_EXPERTS, 1)),
            pl.BlockSpec((1, D_MODEL, D_EXPERT), lambda i, e: (e, 0, 0)),
            pl.BlockSpec((1, D_MODEL, D_EXPERT), lambda i, e: (e, 0, 0)),
            pl.BlockSpec((1, D_EXPERT, D_MODEL), lambda i, e: (e, 0, 0)),
            full2((1, D_MODEL)),
        ],
        out_specs=out_specs,
        out_shape=out_shape,
        scratch_shapes=[pltpu.VMEM((tm, D_MODEL), BF16),
                        pltpu.VMEM((N_EXPERTS, tm), F32), pltpu.VMEM((N_EXPERTS, tm), F32),
                        pltpu.VMEM((N_EXPERTS, tm), F32),
                        pltpu.VMEM((tm, D_MODEL), F32),
                        pltpu.SMEM((N_EXPERTS * (tm // min(MOE_SUB, tm)),), jnp.int32)],
        compiler_params=_cparams("arbitrary", "arbitrary"),
        name="moe_ffn",
    )(*mixer_args, g, wrt_hi, wrt_lo, br_col, wg, wu, wd, gf)


def _rope_tables(max_len):
    pos = jnp.arange(max_len, dtype=F32)
    freqs = ROPE_THETA ** (-(jnp.arange(0, QK_ROPE_DIM, 2, dtype=F32) / QK_ROPE_DIM))
    ang = pos[:, None] * freqs[None, :]
    ang = jnp.concatenate([ang, ang], axis=-1)
    cos, sin = jnp.cos(ang), jnp.sin(ang)
    pad = lambda t: jnp.pad(t, ((0, 0), (QK_NOPE_DIM, HEAD_PAD - QK_HEAD_DIM)))
    return {"cosk": pad(cos), "sink": pad(sin),
            "cosq_t": (cos * Q_SCALE).T, "sinq_t": (sin * Q_SCALE).T}


def _rot_cols(w):
    half = QK_ROPE_DIM // 2
    return jnp.concatenate([-w[..., half:], w[..., :half]], axis=-1)


def _layer_weights(i, p):
    row = lambda v: v.reshape(1, -1)
    w_in = p["w_in"][i]
    w_kr = w_in[:, OFF_KR:]
    place = lambda w: jnp.pad(w, ((0, 0), (QK_NOPE_DIM, HEAD_PAD - QK_HEAD_DIM)))
    w_in_ext = jnp.concatenate([w_in[:, :OFF_KR], place(w_kr), place(_rot_cols(w_kr))], axis=1)
    w_uq = p["w_uq"][i].reshape(Q_LORA_RANK, N_HEADS, QK_HEAD_DIM)
    wq_pad = jnp.pad(w_uq, ((0, 0), (0, 0), (0, HEAD_PAD - QK_HEAD_DIM)))
    wq_t = wq_pad.reshape(Q_LORA_RANK, N_HEADS * HEAD_PAD).T
    wqrot_t = _rot_cols(w_uq[..., QK_NOPE_DIM:]).reshape(Q_LORA_RANK, N_HEADS * QK_ROPE_DIM).T
    w_ukv = p["w_ukv"][i].reshape(KV_LORA_RANK, N_HEADS, QK_NOPE_DIM + V_HEAD_DIM)
    wk = jnp.pad(w_ukv[..., :QK_NOPE_DIM], ((0, 0), (0, 0), (0, HEAD_PAD - QK_NOPE_DIM)))
    wk = wk.reshape(KV_LORA_RANK, N_HEADS * HEAD_PAD)
    wv_t = w_ukv[..., QK_NOPE_DIM:].reshape(KV_LORA_RANK, N_HEADS * V_HEAD_DIM).T
    w_out = p["w_out"][i]
    return {
        "attn_norm_g": row(p["attn_norm_g"][i]),
        "w_in": w_in_ext.astype(BF16),
        "q_norm_g": row(p["q_norm_g"][i]),
        "wq_t": wq_t.astype(BF16),
        "wqrot_t": wqrot_t.astype(BF16),
        "kv_norm_g": row(p["kv_norm_g"][i]),
        "wk": wk.astype(BF16),
        "wv_t": wv_t.astype(BF16),
        "conv_dw_w": p["conv_dw_w"][i].reshape(CONV_WIDTH, D_CONV),
        "conv_dw_b": row(p["conv_dw_b"][i]),
        "conv_ln_g": row(p["conv_ln_g"][i]),
        "conv_ln_b": row(p["conv_ln_b"][i]),
        "conv_out_norm_g": row(p["conv_out_norm_g"][i]),
        "attn_out_norm_g": row(p["attn_out_norm_g"][i]),
        "w_out_a": w_out[:D_ATTN].astype(BF16),
        "w_out_c": w_out[D_ATTN:].astype(BF16),
        "ffn_norm_g": row(p["ffn_norm_g"][i]),
    }


def _split_hi_lo(w):
    hi = w.astype(BF16)
    lo = (w - hi.astype(F32)).astype(BF16)
    return hi, lo


def _tile_sizes(segs):
    s_min = min(s for _, _, s in segs)
    return {
        "tm": min(512, s_min),
        "tc": min(512, s_min),
        "rows": 64,
        "tq": min(512, s_min),
        "tk": min(512, s_min),
        "nb": 8,
        "tf": D_FF // 2,
        "tme": min(1024, s_min),
    }


def kernel(x_prompt, x_sample, attn_norm_g, w_in, q_norm_g, w_uq, kv_norm_g, w_ukv, conv_dw_w,
           conv_dw_b, conv_ln_g, conv_ln_b, attn_out_norm_g, conv_out_norm_g, w_out, ffn_norm_g,
           dense_w_gate, dense_w_up, dense_w_down, moe_w_router, moe_b_router, moe_w_gate,
           moe_w_up, moe_w_down, final_norm_g):
    p = dict(attn_norm_g=attn_norm_g, w_in=w_in, q_norm_g=q_norm_g, w_uq=w_uq,
             kv_norm_g=kv_norm_g, w_ukv=w_ukv, conv_dw_w=conv_dw_w, conv_dw_b=conv_dw_b,
             conv_ln_g=conv_ln_g, conv_ln_b=conv_ln_b, attn_out_norm_g=attn_out_norm_g,
             conv_out_norm_g=conv_out_norm_g, w_out=w_out, ffn_norm_g=ffn_norm_g)
    bp, sp, _ = x_prompt.shape
    bs, ss, _ = x_sample.shape
    n_p, n_s = bp * sp, bs * ss
    segs = ((0, bp, sp), (n_p, bs, ss))
    assert n_p % ss == 0 and n_p % sp == 0
    ts = _tile_sizes(segs)
    tabs = _rope_tables(max(sp, ss))
    x = jnp.concatenate([x_prompt.reshape(n_p, D_MODEL), x_sample.reshape(n_s, D_MODEL)], axis=0)

    assert DEPTH % 2 == 0 and n_p % ts["tme"] == 0
    for i in range(DEPTH):
        lw = _layer_weights(i, p)
        hc, q_t, k_all, v_t = _pre_call(x, lw, tabs, segs, ts["tm"])
        o_t = jnp.zeros((D_ATTN, n_p + n_s), BF16)
        for off, n, s in segs:
            o_t = _attn_call(q_t, k_all, v_t, o_t, off, n, s, ts["tq"], ts["tk"],
                             math.gcd(ts["nb"], max(s // ts["tk"] // 2, 1)))
        ocn = _conv_call(hc, lw, segs, ts["tc"], ts["rows"])
        mixer_args = _mixer_args(o_t, ocn, x, lw)
        j = i // 2
        if i % 2 == 0:
            x = _dense_ffn_call(mixer_args, lw["ffn_norm_g"], dense_w_gate[j].astype(BF16),
                                dense_w_up[j].astype(BF16), dense_w_down[j].astype(BF16),
                                ts["tm"], ts["tf"])
        else:
            wrt = jnp.pad(moe_w_router[j].T, ((0, LANES - N_EXPERTS), (0, 0)))
            wrt_hi, wrt_lo = _split_hi_lo(wrt)
            final = (final_norm_g.reshape(1, D_MODEL), n_p) if i == DEPTH - 1 else None
            x = _moe_call(mixer_args, lw["ffn_norm_g"], wrt_hi, wrt_lo,
                          moe_b_router[j].reshape(N_EXPERTS, 1),
                          moe_w_gate[j].astype(BF16), moe_w_up[j].astype(BF16),
                          moe_w_down[j].astype(BF16), ts["tme"], final)

    y_p, y_s = x
    return (y_p.reshape(bp, sp, D_MODEL), y_s.reshape(bs, ss, D_MODEL))
```
